```python
import math
import jax, jax.numpy as jnp
from jax import lax
import numpy as np


D_MODEL = 1024
BATCH = 2
SEQ = 8192
DEPTH = 4
DEC_BATCH = 128
DEC_SEQ = 8
PAST_LEN = 2048
PAGE_SIZE = 128

HEAD_DIM = 64
H_FOX = (D_MODEL // 2) // HEAD_DIM
FOX_DIM = H_FOX * HEAD_DIM
CONV_DIM = D_MODEL - FOX_DIM
CONV_WIDTH = 3
IN_EVEN = 3 * FOX_DIM + H_FOX + 3 * CONV_DIM
SPLIT_EVEN = (FOX_DIM, 2 * FOX_DIM, 3 * FOX_DIM, 3 * FOX_DIM + H_FOX,
              3 * FOX_DIM + H_FOX + CONV_DIM, 3 * FOX_DIM + H_FOX + 2 * CONV_DIM)
FORGET_BIAS_INIT = 3.0
H_DIFF = D_MODEL // (2 * HEAD_DIM)
DIFF_V = 2 * HEAD_DIM
N_EVEN = (DEPTH + 1) // 2
N_ODD = DEPTH // 2
N_GROUPS = 4
EXPERTS_PER_GROUP = 8
N_EXPERTS = N_GROUPS * EXPERTS_PER_GROUP
TOP_K_INNER = 2
D_EXPERT = D_MODEL // 4
Q_BLOCK = 128
ALPHA = (2 * DEPTH) ** 0.25
BETA = (8 * DEPTH) ** -0.25
LN_EPS = 1e-5
NEG_INF = -1e30

kernel_name = 'hybrid_fox_shortconv_diffattn_hmoe_step'


def _layer_norm(x, g, b):
    xf = x.astype(jnp.float32)
    mu = jnp.mean(xf, -1, keepdims=True)
    var = jnp.mean(jnp.square(xf - mu), -1, keepdims=True)
    return ((xf - mu) * lax.rsqrt(var + LN_EPS) * g + b).astype(x.dtype)


def _rms_norm(x, g):
    xf = x.astype(jnp.float32)
    return (xf * lax.rsqrt(jnp.mean(xf * xf, -1, keepdims=True) + LN_EPS) * g).astype(x.dtype)


def _gather_pages(pool, page_table):
    g = pool[page_table]
    return g.reshape(g.shape[0], g.shape[1] * g.shape[2], *g.shape[3:])


def _query_sweep(block_fn, q_args, q_pos):
    T = q_pos.shape[0]
    if T <= Q_BLOCK or T % Q_BLOCK != 0:
        return block_fn(*q_args, q_pos)
    nb = T // Q_BLOCK

    def split(a):
        return jnp.moveaxis(a.reshape(a.shape[0], nb, Q_BLOCK, *a.shape[2:]), 1, 0)

    out = lax.map(lambda a: block_fn(*a[:-1], a[-1]),
                  (*[split(a) for a in q_args], q_pos.reshape(nb, Q_BLOCK)))
    out = jnp.moveaxis(out, 0, 1)
    return out.reshape(out.shape[0], T, *out.shape[3:])


def _fox_attention(q, k, v, cq, ck, q_pos, k_pos):
    scale = HEAD_DIM ** -0.5
    ck_t = jnp.swapaxes(ck, 1, 2)[:, :, None, :]

    def block(qb, cqb, pos):
        s = jnp.einsum('bqhd,bkhd->bhqk', qb, k).astype(jnp.float32) * scale
        s = s + jnp.swapaxes(cqb, 1, 2)[..., None] - ck_t
        s = jnp.where(k_pos[None, :] <= pos[:, None], s, NEG_INF)
        p = jax.nn.softmax(s, axis=-1).astype(v.dtype)
        return jnp.einsum('bhqk,bkhd->bqhd', p, v)

    return _query_sweep(block, (q, cq), q_pos)


def _diff_attention(q, k, v, lam, q_pos, k_pos):
    scale = HEAD_DIM ** -0.5

    def block(qb, pos):
        s = jnp.einsum('bqhid,bkhid->bhiqk', qb, k).astype(jnp.float32) * scale
        s = jnp.where(k_pos[None, :] <= pos[:, None], s, NEG_INF)
        p = jax.nn.softmax(s, axis=-1)
        a = (p[:, :, 0] - lam * p[:, :, 1]).astype(v.dtype)
        return jnp.einsum('bhqk,bkhe->bqhe', a, v)

    return _query_sweep(block, (q,), q_pos)


def _causal_conv3(u, buf, w):
    T = u.shape[1]
    up = jnp.concatenate([buf, u], axis=1)
    y = sum(w[j] * up[:, j:j + T] for j in range(CONV_WIDTH))
    return y, up[:, up.shape[1] - (CONV_WIDTH - 1):]


def _even_mixer(h, w_in, b_f, conv_w, w_out, past):
    B, T, _ = h.shape
    q, k, v, f, gate_b, gate_c, u = jnp.split(h @ w_in, SPLIT_EVEN, axis=-1)
    q = q.reshape(B, T, H_FOX, HEAD_DIM)
    k = k.reshape(B, T, H_FOX, HEAD_DIM)
    v = v.reshape(B, T, H_FOX, HEAD_DIM)
    logf = jax.nn.log_sigmoid((f + b_f).astype(jnp.float32))
    if past is None:
        k_all, v_all, logf_all, P = k, v, logf, 0
        conv_buf = jnp.zeros((B, CONV_WIDTH - 1, CONV_DIM), u.dtype)
    else:
        k_past, v_past, logf_past, conv_buf = past
        P = k_past.shape[1]
        k_all = jnp.concatenate([k_past, k], axis=1)
        v_all = jnp.concatenate([v_past, v], axis=1)
        logf_all = jnp.concatenate([logf_past.astype(jnp.float32), logf], axis=1)
    c_all = jnp.cumsum(logf_all, axis=1)
    q_pos = P + jnp.arange(T)
    k_pos = jnp.arange(P + T)
    o_fox = _fox_attention(q, k_all, v_all, c_all[:, P:], c_all, q_pos, k_pos).reshape(B, T, FOX_DIM)
    conv_out, new_buf = _causal_conv3(gate_c * u, conv_buf, conv_w)
    o_conv = gate_b * conv_out
    y = jnp.concatenate([o_fox, o_conv], axis=-1) @ w_out
    return y, (k, v, logf, new_buf)


def _odd_mixer(h, w_qkv, lq1, lk1, lq2, lk2, subln_g, w_out, lambda_init, past):
    B, T, _ = h.shape
    q, k, v = jnp.split(h @ w_qkv, 3, axis=-1)
    q = q.reshape(B, T, H_DIFF, 2, HEAD_DIM)
    k = k.reshape(B, T, 2 * H_DIFF, HEAD_DIM)
    v = v.reshape(B, T, H_DIFF, DIFF_V)
    if past is None:
        k_all, v_all, P = k, v, 0
    else:
        k_past, v_past = past
        P = k_past.shape[1]
        k_all = jnp.concatenate([k_past, k], axis=1)
        v_all = jnp.concatenate([v_past, v], axis=1)
    f32 = jnp.float32
    lam = (jnp.exp(jnp.sum(lq1.astype(f32) * lk1.astype(f32)))
           - jnp.exp(jnp.sum(lq2.astype(f32) * lk2.astype(f32))) + lambda_init)
    q_pos = P + jnp.arange(T)
    k_pos = jnp.arange(P + T)
    o = _diff_attention(q, k_all.reshape(B, P + T, H_DIFF, 2, HEAD_DIM), v_all, lam, q_pos, k_pos)
    o = _rms_norm(o, subln_g) * (1.0 - lambda_init)
    y = o.reshape(B, T, D_MODEL) @ w_out
    return y, (k, v)


def _hier_moe(h, w_gr, b_gr, w_er, b_er, w1, w3, w2):
    B, T, D = h.shape
    xt = h.reshape(B * T, D)
    f32 = jnp.float32
    g_logits = (xt @ w_gr).astype(f32) + b_gr
    g_prob = jax.nn.softmax(g_logits, axis=-1)
    _, g_idx = lax.top_k(g_logits, 1)
    onehot_g = jax.nn.one_hot(g_idx[:, 0], N_GROUPS, dtype=f32)
    g_w = jnp.sum(g_prob * onehot_g, axis=-1)
    e_all = jnp.einsum('nd,gde->nge', xt, w_er).astype(f32) + b_er
    e_logits = jnp.einsum('nge,ng->ne', e_all, onehot_g)
    e_val, e_idx = lax.top_k(e_logits, TOP_K_INNER)
    e_w = jax.nn.softmax(e_val, axis=-1)
    inner = jnp.sum(e_w[..., None] * jax.nn.one_hot(e_idx, EXPERTS_PER_GROUP, dtype=f32), axis=1)
    gates = (g_w[:, None, None] * onehot_g[:, :, None] * inner[:, None, :]).reshape(B * T, N_EXPERTS)

    def step(acc, xs):
        w1e, w3e, w2e, ge = xs
        hid = jax.nn.silu(xt @ w1e) * (xt @ w3e)
        return acc + ge[:, None] * (hid @ w2e).astype(f32), None

    y, _ = lax.scan(step, jnp.zeros((B * T, D), f32), (w1, w3, w2, gates.T))
    return y.astype(h.dtype).reshape(B, T, D)


def _trunk(x, c, past, weights):
    (ada_w, ada_b, ln_g, ln_b, w_in_even, b_forget, conv_w, w_out_even,
     w_qkv_odd, lambda_q1, lambda_k1, lambda_q2, lambda_k2, subln_g, w_out_odd,
     w_gr, b_gr, w_er, b_er, w1, w3, w2) = weights
    fk, fv, fl, cb, dk, dv = [], [], [], [], [], []
    for l in range(DEPTH):
        mod = (jax.nn.silu(c) @ ada_w[l] + ada_b[l])[:, None, :]
        sh1, sc1, g1, sh2, sc2, g2 = jnp.split(mod, 6, axis=-1)
        h = x * (1 + sc1) + sh1
        i = l // 2
        if l % 2 == 0:
            layer_past = None
            if past is not None:
                c_fk, c_fv, c_fl, s_conv, _, _, pt = past
                layer_past = (_gather_pages(c_fk[i], pt), _gather_pages(c_fv[i], pt),
                              _gather_pages(c_fl[i], pt), s_conv[i])
            y, (k, v, logf, buf) = _even_mixer(h, w_in_even[i], b_forget[i], conv_w[i],
                                               w_out_even[i], layer_past)
            fk.append(k)
            fv.append(v)
            fl.append(logf)
            cb.append(buf)
        else:
            layer_past = None
            if past is not None:
                _, _, _, _, c_dk, c_dv, pt = past
                layer_past = (_gather_pages(c_dk[i], pt), _gather_pages(c_dv[i], pt))
            lambda_init = 0.8 - 0.6 * math.exp(-0.3 * l)
            y, (k, v) = _odd_mixer(h, w_qkv_odd[i], lambda_q1[i], lambda_k1[i], lambda_q2[i],
                                   lambda_k2[i], subln_g[i], w_out_odd[i], lambda_init, layer_past)
            dk.append(k)
            dv.append(v)
        x = _layer_norm(ALPHA * x + (1 + g1) * y, ln_g[l, 0], ln_b[l, 0])
        h = x * (1 + sc2) + sh2
        ffn = _hier_moe(h, w_gr[l], b_gr[l], w_er[l], b_er[l], w1[l], w3[l], w2[l])
        x = _layer_norm(ALPHA * x + (1 + g2) * ffn, ln_g[l, 1], ln_b[l, 1])
    return (x, jnp.stack(fk), jnp.stack(fv), jnp.stack(fl), jnp.stack(cb), jnp.stack(dk), jnp.stack(dv))


def setup_inputs(seed: int = 0) -> dict:
    key = jax.random.key(seed)
    ks = list(jax.random.split(key, 40))

    def nrm(i, shape, s=1.0):
        return s * jax.random.normal(ks[i], shape, jnp.float32)

    n_pages = PAST_LEN // PAGE_SIZE
    n_used = DEC_BATCH * n_pages
    n_phys = n_used + n_used // 4
    d = D_MODEL
    ws = d ** -0.5
    col_even = jnp.concatenate([jnp.ones((2 * FOX_DIM,)), jnp.full((FOX_DIM,), BETA),
                                jnp.ones((H_FOX + 2 * CONV_DIM,)), jnp.full((CONV_DIM,), BETA)]).astype(jnp.float32)
    col_odd = jnp.concatenate([jnp.ones((2 * d,)), jnp.full((d,), BETA)]).astype(jnp.float32)
    page_table = jax.random.permutation(ks[8], n_phys)[:n_used].reshape(DEC_BATCH, n_pages).astype(jnp.int32)
    return {
        'x_prompt': nrm(0, (BATCH, SEQ, d)),
        'x_sample': nrm(1, (DEC_BATCH, DEC_SEQ, d)),
        'cache_fox_k': nrm(2, (N_EVEN, n_phys, PAGE_SIZE, H_FOX, HEAD_DIM)),
        'cache_fox_v': nrm(3, (N_EVEN, n_phys, PAGE_SIZE, H_FOX, HEAD_DIM), BETA),
        'cache_fox_logf': jax.nn.log_sigmoid(FORGET_BIAS_INIT + nrm(4, (N_EVEN, n_phys, PAGE_SIZE, H_FOX))),
        'state_conv': nrm(5, (N_EVEN, DEC_BATCH, CONV_WIDTH - 1, CONV_DIM), BETA),
        'cache_diff_k': nrm(6, (N_ODD, n_phys, PAGE_SIZE, 2 * H_DIFF, HEAD_DIM)),
        'cache_diff_v': nrm(7, (N_ODD, n_phys, PAGE_SIZE, H_DIFF, DIFF_V), BETA),
        'page_table': page_table,
        'c_prompt': nrm(9, (BATCH, d)),
        'c_sample': nrm(10, (DEC_BATCH, d)),
        'ada_w': nrm(11, (DEPTH, d, 6 * d), 0.1 * ws),
        'ada_b': nrm(12, (DEPTH, 6 * d), 0.01),
        'ln_g': 1.0 + nrm(13, (DEPTH, 2, d), 0.02),
        'ln_b': nrm(14, (DEPTH, 2, d), 0.02),
        'w_in_even': nrm(15, (N_EVEN, d, IN_EVEN), ws) * col_even,
        'b_forget': FORGET_BIAS_INIT + nrm(16, (N_EVEN, H_FOX), 0.5),
        'conv_w': nrm(17, (N_EVEN, CONV_WIDTH, CONV_DIM), CONV_WIDTH ** -0.5),
        'w_out_even': nrm(18, (N_EVEN, d, d), ws * BETA),
        'w_qkv_odd': nrm(19, (N_ODD, d, 3 * d), ws) * col_odd,
        'lambda_q1': nrm(20, (N_ODD, HEAD_DIM), 0.1),
        'lambda_k1': nrm(21, (N_ODD, HEAD_DIM), 0.1),
        'lambda_q2': nrm(22, (N_ODD, HEAD_DIM), 0.1),
        'lambda_k2': nrm(23, (N_ODD, HEAD_DIM), 0.1),
        'subln_g': 1.0 + nrm(24, (N_ODD, DIFF_V), 0.02),
        'w_out_odd': nrm(25, (N_ODD, d, d), ws * BETA),
        'w_gr': nrm(26, (DEPTH, d, N_GROUPS), ws),
        'b_gr': nrm(27, (DEPTH, N_GROUPS), 0.01),
        'w_er': nrm(28, (DEPTH, N_GROUPS, d, EXPERTS_PER_GROUP), ws),
        'b_er': nrm(29, (DEPTH, N_GROUPS, EXPERTS_PER_GROUP), 0.01),
        'w1': nrm(30, (DEPTH, N_EXPERTS, d, D_EXPERT), ws),
        'w3': nrm(31, (DEPTH, N_EXPERTS, d, D_EXPERT), ws),
        'w2': nrm(32, (DEPTH, N_EXPERTS, D_EXPERT, d), (D_EXPERT ** -0.5) * BETA),
    }


def reference(x_prompt, x_sample, cache_fox_k, cache_fox_v, cache_fox_logf, state_conv,
              cache_diff_k, cache_diff_v, page_table, c_prompt, c_sample,
              ada_w, ada_b, ln_g, ln_b, w_in_even, b_forget, conv_w, w_out_even,
              w_qkv_odd, lambda_q1, lambda_k1, lambda_q2, lambda_k2, subln_g, w_out_odd,
              w_gr, b_gr, w_er, b_er, w1, w3, w2):
    weights = (ada_w, ada_b, ln_g, ln_b, w_in_even, b_forget, conv_w, w_out_even,
               w_qkv_odd, lambda_q1, lambda_k1, lambda_q2, lambda_k2, subln_g, w_out_odd,
               w_gr, b_gr, w_er, b_er, w1, w3, w2)
    y_prompt, fk_p, fv_p, fl_p, cb_p, dk_p, dv_p = _trunk(x_prompt, c_prompt, None, weights)
    past = (cache_fox_k, cache_fox_v, cache_fox_logf, state_conv, cache_diff_k, cache_diff_v, page_table)
    y_sample, fk_s, fv_s, fl_s, cb_s, dk_s, dv_s = _trunk(x_sample, c_sample, past, weights)
    return (y_prompt, y_sample, fk_p, fv_p, fl_p, cb_p, dk_p, dv_p, fk_s, fv_s, fl_s, cb_s, dk_s, dv_s)
```

```python
import functools
import math

import jax
import jax.numpy as jnp
from jax import lax
from jax.experimental import pallas as pl
from jax.experimental.pallas import tpu as pltpu

F32, BF16, I32 = jnp.float32, jnp.bfloat16, jnp.int32
HI = lax.Precision.HIGHEST

D_MODEL = 1024
DEPTH = 4
HEAD_DIM = 64
N_HEADS = 8
FOX_DIM = N_HEADS * HEAD_DIM
CONV_DIM = D_MODEL - FOX_DIM
CONV_WIDTH = 3
N_GROUPS = 4
EXPERTS_PER_GROUP = 8
D_EXPERT = D_MODEL // 4
ALPHA = (2 * DEPTH) ** 0.25
LN_EPS = 1e-5
NEG_INF = -1e30
QK_SCALE = HEAD_DIM ** -0.5

LANES = 128
SUBLANES = 8
VMEM_LIMIT = 56 * 1024 * 1024
ROUTER_LANES = LANES
EXPERT_LANE0 = N_GROUPS
MOE_TILE = 256

_NT = (((1,), (1,)), ((), ()))


def _params(n_grid):
    return pltpu.CompilerParams(dimension_semantics=("arbitrary",) * n_grid, vmem_limit_bytes=VMEM_LIMIT)


def _row_tile(n):
    return min(512, n)


def _mod_spec(mod, chunk, tm, rows_per_seq):
    if mod.ndim == 3:
        tiles_per_seq = rows_per_seq // tm
        return pl.BlockSpec((None, 1, D_MODEL), lambda i: (i // tiles_per_seq, 0, chunk))
    return pl.BlockSpec((tm, D_MODEL), lambda i: (i, chunk))


def _const_spec(shape):
    return pl.BlockSpec(shape, lambda *_: (0,) * len(shape))


def _layer_norm(z, g, b):
    mu = jnp.mean(z, axis=-1, keepdims=True)
    zc = z - mu
    var = jnp.mean(zc * zc, axis=-1, keepdims=True)
    return zc * lax.rsqrt(var + LN_EPS) * g + b


def _log_sigmoid(x):
    return jnp.minimum(x, 0.0) - jnp.log1p(jnp.exp(-jnp.abs(x)))


def _silu(x):
    return x * jax.nn.sigmoid(x)


def _adaln_kernel(c_ref, w_ref, b_ref, o_ref):
    a = _silu(c_ref[...]).astype(BF16)
    o_ref[...] = jnp.dot(a, w_ref[...].astype(BF16), preferred_element_type=F32) + b_ref[...]


def _adaln(c_all, ada_w, ada_b):
    rows = c_all.shape[0]
    tn = 1536
    return pl.pallas_call(
        _adaln_kernel,
        grid=(DEPTH, 6 * D_MODEL // tn),
        in_specs=[pl.BlockSpec((rows, D_MODEL), lambda l, j: (0, 0)),
                  pl.BlockSpec((None, D_MODEL, tn), lambda l, j: (l, 0, j)),
                  pl.BlockSpec((None, 1, tn), lambda l, j: (l, 0, j))],
        out_specs=pl.BlockSpec((None, rows, tn), lambda l, j: (l, 0, j)),
        out_shape=jax.ShapeDtypeStruct((DEPTH, rows, 6 * D_MODEL), F32),
        compiler_params=_params(2), name="adaln",
    )(c_all, ada_w, ada_b.reshape(DEPTH, 1, 6 * D_MODEL))


def _even_in_kernel(x_ref, sc_ref, sh_ref, wm_ref, wf_ref, bf_ref,
                    qb_ref, kb_ref, vb_ref, k_ref, v_ref, gb_ref, cu_ref, lf_ref):
    h = (x_ref[...] * (1.0 + sc_ref[...]) + sh_ref[...]).astype(BF16)

    def mm(c):
        return jnp.dot(h, wm_ref[:, c * FOX_DIM:(c + 1) * FOX_DIM], preferred_element_type=F32)

    qb_ref[...] = (mm(0) * QK_SCALE).astype(BF16)
    k = mm(1)
    k_ref[...] = k
    kb_ref[...] = k.astype(BF16)
    v = mm(2)
    v_ref[...] = v
    vb_ref[...] = v.astype(BF16)
    gb_ref[...] = mm(3)
    cu_ref[...] = mm(4) * mm(5)
    f = jnp.dot(h, wf_ref[...], preferred_element_type=F32) + bf_ref[...]
    lane = lax.broadcasted_iota(I32, f.shape, 1)
    lf_ref[...] = jnp.where(lane < N_HEADS, _log_sigmoid(f), 0.0)


def _even_in(x, mod, rows_per_seq, wm, wf, bfp):
    n = x.shape[0]
    tm = _row_tile(n)
    row = lambda w: pl.BlockSpec((tm, w), lambda i: (i, 0))
    sds = lambda w, dt: jax.ShapeDtypeStruct((n, w), dt)
    return pl.pallas_call(
        _even_in_kernel,
        grid=(n // tm,),
        in_specs=[row(D_MODEL), _mod_spec(mod, 1, tm, rows_per_seq), _mod_spec(mod, 0, tm, rows_per_seq),
                  _const_spec(wm.shape), _const_spec(wf.shape), _const_spec(bfp.shape)],
        out_specs=[row(FOX_DIM)] * 7 + [row(LANES)],
        out_shape=[sds(FOX_DIM, BF16)] * 3 + [sds(FOX_DIM, F32)] * 4 + [sds(LANES, F32)],
        compiler_params=_params(1), name="even_in",
    )(x, mod, mod, wm, wf, bfp)


def _odd_in_kernel(x_ref, sc_ref, sh_ref, w_ref, qb_ref, kb_ref, vb_ref, k_ref, v_ref):
    h = (x_ref[...] * (1.0 + sc_ref[...]) + sh_ref[...]).astype(BF16)

    def mm(c):
        return jnp.dot(h, w_ref[:, c * D_MODEL:(c + 1) * D_MODEL], preferred_element_type=F32)

    qb_ref[...] = (mm(0) * QK_SCALE).astype(BF16)
    k = mm(1)
    k_ref[...] = k
    kb_ref[...] = k.astype(BF16)
    v = mm(2)
    v_ref[...] = v
    vb_ref[...] = v.astype(BF16)


def _odd_in(x, mod, rows_per_seq, w):
    n = x.shape[0]
    tm = _row_tile(n)
    row = pl.BlockSpec((tm, D_MODEL), lambda i: (i, 0))
    sds = lambda dt: jax.ShapeDtypeStruct((n, D_MODEL), dt)
    return pl.pallas_call(
        _odd_in_kernel,
        grid=(n // tm,),
        in_specs=[row, _mod_spec(mod, 1, tm, rows_per_seq), _mod_spec(mod, 0, tm, rows_per_seq), _const_spec(w.shape)],
        out_specs=[row] * 5,
        out_shape=[sds(BF16)] * 3 + [sds(F32)] * 2,
        compiler_params=_params(1), name="odd_in",
    )(x, mod, mod, w)


def _cumsum_kernel(lf_ref, logf_ref, ctok_ref, ct_ref, carry):
    @pl.when(pl.program_id(1) == 0)
    def _():
        carry[...] = jnp.zeros_like(carry)

    lf = lf_ref[...]
    tc = lf.shape[0]
    r = lax.broadcasted_iota(I32, (tc, tc), 0)
    c = lax.broadcasted_iota(I32, (tc, tc), 1)
    tri = (c <= r).astype(F32)
    cs = jnp.dot(tri, lf, precision=HI, preferred_element_type=F32) + carry[...]
    carry[...] = cs[tc - 1:tc, :]
    logf_ref[...] = lf[:, :N_HEADS]
    ctok_ref[...] = cs
    eye = (lax.broadcasted_iota(I32, (N_HEADS, LANES), 0) == lax.broadcasted_iota(I32, (N_HEADS, LANES), 1)).astype(F32)
    ct_ref[...] = lax.dot_general(eye, cs, _NT, precision=HI, preferred_element_type=F32)


def _cumsum(lf, batch, seq):
    n = lf.shape[0]
    tc = min(512, seq)
    nt = seq // tc
    return pl.pallas_call(
        _cumsum_kernel,
        grid=(batch, nt),
        in_specs=[pl.BlockSpec((tc, LANES), lambda b, t: (b * nt + t, 0))],
        out_specs=[pl.BlockSpec((tc, N_HEADS), lambda b, t: (b * nt + t, 0)),
                   pl.BlockSpec((tc, LANES), lambda b, t: (b * nt + t, 0)),
                   pl.BlockSpec((None, N_HEADS, tc), lambda b, t: (b, 0, t))],
        out_shape=[jax.ShapeDtypeStruct((n, N_HEADS), F32), jax.ShapeDtypeStruct((n, LANES), F32),
                   jax.ShapeDtypeStruct((batch, N_HEADS, seq), F32)],
        scratch_shapes=[pltpu.VMEM((1, LANES), F32)],
        compiler_params=_params(2), name="logf_cumsum",
    )(lf)


def _online_update(state, s, v):
    m, l, acc = state
    m_new = jnp.maximum(m, jnp.max(s, axis=1, keepdims=True))
    alpha = jnp.exp(m - m_new)
    p = jnp.exp(s - m_new)
    l = alpha * l + jnp.sum(p, axis=1, keepdims=True)
    acc = alpha * acc + jnp.dot(p.astype(BF16), v, preferred_element_type=F32)
    return m_new, l, acc


def _flash_kernel(fox, lam_init, tile, *refs):
    if fox:
        q_ref, k_ref, v_ref, ctok_ref, ck_ref, o_ref = refs
    else:
        q_ref, k_ref, v_ref, lq1_ref, lk1_ref, lq2_ref, lk2_ref, g_ref, o_ref = refs
    i = pl.program_id(2)
    q = q_ref[...]
    lane = lax.broadcasted_iota(I32, q.shape, 1)
    low = lane < HEAD_DIM
    zero = jnp.zeros_like(q)
    q_a = jnp.where(low, q, zero)
    q_b = jnp.where(low, zero, q)
    if fox:
        hp = pl.program_id(1)
        ctok = ctok_ref[...]
        cq_a = jnp.sum(jnp.where(lane == 2 * hp, ctok, 0.0), axis=1, keepdims=True)
        cq_b = jnp.sum(jnp.where(lane == 2 * hp + 1, ctok, 0.0), axis=1, keepdims=True)

    def scores(j, diagonal):
        start = pl.multiple_of(j * tile, tile)
        kj = k_ref[pl.ds(start, tile), :]
        vj = v_ref[pl.ds(start, tile), :]
        s_a = lax.dot_general(q_a, kj, _NT, preferred_element_type=F32)
        s_b = lax.dot_general(q_b, kj, _NT, preferred_element_type=F32)
        if fox:
            s_a = s_a + cq_a - ck_ref[0:1, pl.ds(start, tile)]
            s_b = s_b + cq_b - ck_ref[1:2, pl.ds(start, tile)]
        if diagonal:
            visible = lax.broadcasted_iota(I32, s_a.shape, 1) <= lax.broadcasted_iota(I32, s_a.shape, 0)
            s_a = jnp.where(visible, s_a, NEG_INF)
            s_b = jnp.where(visible, s_b, NEG_INF)
        return s_a, s_b, vj

    def step(j, state, diagonal=False):
        s_a, s_b, vj = scores(j, diagonal)
        return _online_update(state[0], s_a, vj), _online_update(state[1], s_b, vj)

    tq = q.shape[0]
    init = (jnp.full((tq, 1), NEG_INF, F32), jnp.zeros((tq, 1), F32), jnp.zeros((tq, LANES), F32))
    state = lax.fori_loop(0, i, step, (init, init))
    (_, l_a, acc_a), (_, l_b, acc_b) = step(i, state, diagonal=True)
    if fox:
        o_ref[...] = jnp.where(low, acc_a / l_a, acc_b / l_b)
    else:
        lam = (jnp.exp(jnp.sum(lq1_ref[...] * lk1_ref[...], axis=1, keepdims=True))
               - jnp.exp(jnp.sum(lq2_ref[...] * lk2_ref[...], axis=1, keepdims=True)) + lam_init)
        o = acc_a / l_a - lam * (acc_b / l_b)
        ms = jnp.mean(o * o, axis=1, keepdims=True)
        o_ref[...] = o * lax.rsqrt(ms + LN_EPS) * g_ref[...] * (1.0 - lam_init)


def _flash(qb, kb, vb, batch, seq, *, fox, extra, lam_init=0.0):
    n, width = qb.shape
    tile = min(512, seq)
    nq = seq // tile
    blocks = width // LANES
    qspec = pl.BlockSpec((tile, LANES), lambda b, h, i: (b * nq + i, h))
    kvspec = pl.BlockSpec((seq, LANES), lambda b, h, i: (b, h))
    if fox:
        ctok, ct = extra
        especs = [pl.BlockSpec((tile, LANES), lambda b, h, i: (b * nq + i, 0)),
                  pl.BlockSpec((None, None, 2, seq), lambda b, h, i: (b, h, 0, 0))]
        eargs = [ctok, ct.reshape(batch, blocks, 2, seq)]
    else:
        especs = [_const_spec(a.shape) for a in extra]
        eargs = list(extra)
    return pl.pallas_call(
        functools.partial(_flash_kernel, fox, lam_init, tile),
        grid=(batch, blocks, nq),
        in_specs=[qspec, kvspec, kvspec] + especs,
        out_specs=qspec,
        out_shape=jax.ShapeDtypeStruct((n, width), F32),
        compiler_params=_params(3), name="flash_fox" if fox else "flash_diff",
    )(qb, kb, vb, *eargs)


def _same_head(shape, row_div=1, col_div=1):
    r = lax.broadcasted_iota(I32, shape, 0) // row_div
    c = lax.broadcasted_iota(I32, shape, 1) // col_div
    return (r % N_HEADS) == (c % N_HEADS)


def _fox_dec_kernel(n_pages, pt_ref, q_ref, kn_ref, vn_ref, ln_ref, kc_ref, vc_ref, lp_ref, o_ref,
                    m_sc, l_sc, acc_sc, suf_sc, cn_sc):
    s = pl.program_id(1)
    q = q_ref[...]
    rows = q.shape[0]
    page_cols = lp_ref.shape[1]

    @pl.when(s == 0)
    def _():
        x = ln_ref[...]
        lane = lax.broadcasted_iota(I32, x.shape, 1)
        sh = N_HEADS
        while sh < rows:
            x = x + jnp.where(lane >= sh, pltpu.roll(x, sh, axis=1), 0.0)
            sh *= 2
        diag = lax.broadcasted_iota(I32, (rows, LANES), 0) == lax.broadcasted_iota(I32, (rows, LANES), 1)
        cn_col = jnp.sum(jnp.where(diag, x, 0.0), axis=1, keepdims=True)
        cn_sc[...] = cn_col
        sc = lax.dot_general(q, kn_ref[...], _NT, preferred_element_type=F32)
        sc = sc + cn_col - x[:, :rows]
        r = lax.broadcasted_iota(I32, sc.shape, 0)
        c = lax.broadcasted_iota(I32, sc.shape, 1)
        visible = _same_head(sc.shape) & ((c // N_HEADS) <= (r // N_HEADS))
        sc = jnp.where(visible, sc, NEG_INF)
        m = jnp.max(sc, axis=1, keepdims=True)
        p = jnp.exp(sc - m)
        m_sc[...] = m
        l_sc[...] = jnp.sum(p, axis=1, keepdims=True)
        acc_sc[...] = jnp.dot(p.astype(BF16), vn_ref[...], preferred_element_type=F32)
        suf_sc[...] = jnp.zeros_like(suf_sc)

    @pl.when(s > 0)
    def _():
        lp = lp_ref[...]
        lane = lax.broadcasted_iota(I32, lp.shape, 1)
        x, tot = lp, lp
        sh = N_HEADS
        while sh < page_cols:
            x = x + jnp.where(lane + sh < page_cols, pltpu.roll(x, page_cols - sh, axis=1), 0.0)
            tot = tot + pltpu.roll(tot, sh, axis=1)
            sh *= 2
        later = suf_sc[...] + (x - lp)
        suf_sc[...] = suf_sc[...] + tot
        k2 = kc_ref[...].reshape(page_cols, HEAD_DIM).astype(BF16)
        v2 = vc_ref[...].reshape(page_cols, HEAD_DIM).astype(BF16)
        sc = lax.dot_general(q, k2, _NT, preferred_element_type=F32)
        sc = sc + cn_sc[...] + later
        sc = jnp.where(_same_head(sc.shape), sc, NEG_INF)
        m, l, acc = _online_update((m_sc[...], l_sc[...], acc_sc[...]), sc, v2)
        m_sc[...] = m
        l_sc[...] = l
        acc_sc[...] = acc

    @pl.when(s == n_pages)
    def _():
        o_ref[...] = acc_sc[...] / l_sc[...]


def _fox_decode(layer, pt_flat, n_pages, q2, kn2, vn2, lnew, cache_k, cache_v, logf_pages):
    nb, rows, _ = q2.shape
    page, heads = cache_k.shape[2], cache_k.shape[3]
    page_cols = page * heads

    def page_of(b, s, pt):
        return pt[b * n_pages + jnp.minimum(n_pages - s, n_pages - 1)]

    seq_spec = lambda r, w: pl.BlockSpec((None, r, w), lambda b, s, pt: (b, 0, 0))
    cache_spec = pl.BlockSpec((None, None, page, heads, HEAD_DIM), lambda b, s, pt: (layer, page_of(b, s, pt), 0, 0, 0))
    gs = pltpu.PrefetchScalarGridSpec(
        num_scalar_prefetch=1, grid=(nb, n_pages + 1),
        in_specs=[seq_spec(rows, HEAD_DIM), seq_spec(rows, HEAD_DIM), seq_spec(rows, HEAD_DIM), seq_spec(1, LANES),
                  cache_spec, cache_spec,
                  pl.BlockSpec((None, None, 1, page_cols), lambda b, s, pt: (layer, page_of(b, s, pt), 0, 0))],
        out_specs=seq_spec(rows, HEAD_DIM),
        scratch_shapes=[pltpu.VMEM((rows, 1), F32), pltpu.VMEM((rows, 1), F32), pltpu.VMEM((rows, HEAD_DIM), F32),
                        pltpu.VMEM((1, page_cols), F32), pltpu.VMEM((rows, 1), F32)])
    return pl.pallas_call(
        functools.partial(_fox_dec_kernel, n_pages), grid_spec=gs,
        out_shape=jax.ShapeDtypeStruct((nb, rows, HEAD_DIM), F32),
        compiler_params=_params(2), name="fox_decode",
    )(pt_flat, q2, kn2, vn2, lnew, cache_k, cache_v, logf_pages)


def _diff_dec_kernel(n_pages, lam_init, pt_ref, q_ref, kn_ref, vn_ref, kc_ref, vc_ref,
                     lq1_ref, lk1_ref, lq2_ref, lk2_ref, g_ref, o_ref, m_sc, l_sc, acc_sc):
    s = pl.program_id(1)

    @pl.when(s == 0)
    def _():
        vn = vn_ref[...]
        for half in range(2):
            sc = lax.dot_general(q_ref[half], kn_ref[half], _NT, preferred_element_type=F32)
            r = lax.broadcasted_iota(I32, sc.shape, 0)
            c = lax.broadcasted_iota(I32, sc.shape, 1)
            visible = _same_head(sc.shape) & ((c // N_HEADS) <= (r // N_HEADS))
            sc = jnp.where(visible, sc, NEG_INF)
            m = jnp.max(sc, axis=1, keepdims=True)
            p = jnp.exp(sc - m)
            m_sc[half] = m
            l_sc[half] = jnp.sum(p, axis=1, keepdims=True)
            acc_sc[half] = jnp.dot(p.astype(BF16), vn, preferred_element_type=F32)

    @pl.when(s > 0)
    def _():
        page, heads2 = kc_ref.shape[0], kc_ref.shape[1]
        cols = page * heads2 // 2
        v2 = vc_ref[...].reshape(cols, 2 * HEAD_DIM).astype(BF16)
        for half in range(2):
            k2 = kc_ref[:, pl.ds(half, heads2 // 2, stride=2), :].reshape(cols, HEAD_DIM).astype(BF16)
            sc = lax.dot_general(q_ref[half], k2, _NT, preferred_element_type=F32)
            sc = jnp.where(_same_head(sc.shape), sc, NEG_INF)
            m, l, acc = _online_update((m_sc[half], l_sc[half], acc_sc[half]), sc, v2)
            m_sc[half] = m
            l_sc[half] = l
            acc_sc[half] = acc

    @pl.when(s == n_pages)
    def _():
        lam = (jnp.exp(jnp.sum(lq1_ref[...] * lk1_ref[...], axis=1, keepdims=True))
               - jnp.exp(jnp.sum(lq2_ref[...] * lk2_ref[...], axis=1, keepdims=True)) + lam_init)
        o = acc_sc[0] / l_sc[0] - lam * (acc_sc[1] / l_sc[1])
        ms = jnp.mean(o * o, axis=1, keepdims=True)
        o_ref[...] = o * lax.rsqrt(ms + LN_EPS) * g_ref[...] * (1.0 - lam_init)


def _diff_decode(layer, pt_flat, n_pages, q2, kn2, vn2, cache_k, cache_v, lam_vecs, subln, lam_init):
    nb, _, rows, _ = q2.shape
    page = cache_k.shape[2]
    dv = 2 * HEAD_DIM

    def page_of(b, s, pt):
        return pt[b * n_pages + jnp.minimum(n_pages - s, n_pages - 1)]

    half_spec = pl.BlockSpec((None, 2, rows, HEAD_DIM), lambda b, s, pt: (b, 0, 0, 0))
    gs = pltpu.PrefetchScalarGridSpec(
        num_scalar_prefetch=1, grid=(nb, n_pages + 1),
        in_specs=[half_spec, half_spec, pl.BlockSpec((None, rows, dv), lambda b, s, pt: (b, 0, 0)),
                  pl.BlockSpec((None, None, page, 2 * N_HEADS, HEAD_DIM), lambda b, s, pt: (layer, page_of(b, s, pt), 0, 0, 0)),
                  pl.BlockSpec((None, None, page, N_HEADS, dv), lambda b, s, pt: (layer, page_of(b, s, pt), 0, 0, 0))]
                 + [_const_spec(a.shape) for a in lam_vecs] + [_const_spec(subln.shape)],
        out_specs=pl.BlockSpec((None, rows, dv), lambda b, s, pt: (b, 0, 0)),
        scratch_shapes=[pltpu.VMEM((2, rows, 1), F32), pltpu.VMEM((2, rows, 1), F32), pltpu.VMEM((2, rows, dv), F32)])
    return pl.pallas_call(
        functools.partial(_diff_dec_kernel, n_pages, lam_init), grid_spec=gs,
        out_shape=jax.ShapeDtypeStruct((nb, rows, dv), F32),
        compiler_params=_params(2), name="diff_decode",
    )(pt_flat, q2, kn2, vn2, cache_k, cache_v, *lam_vecs, subln)


def _conv_kernel(cu_ref, gb_ref, buf_ref, w_ref, o_ref, nb_ref, carry):
    @pl.when(pl.program_id(1) == 0)
    def _():
        carry[...] = buf_ref[...]

    cu = cu_ref[...]
    rows = cu.shape[1]
    t = lax.broadcasted_iota(I32, cu.shape, 1)
    prev1 = carry[:, 1:2, :]
    prev2 = carry[:, 0:1, :]
    back1 = jnp.where(t == 0, prev1, pltpu.roll(cu, 1, axis=1))
    back2 = jnp.where(t == 0, prev2, jnp.where(t == 1, prev1, pltpu.roll(cu, 2, axis=1)))
    y = w_ref[0:1, :] * back2 + w_ref[1:2, :] * back1 + w_ref[2:3, :] * cu
    o_ref[...] = gb_ref[...] * y
    last = cu[:, rows - (CONV_WIDTH - 1):, :]
    carry[...] = last
    nb_ref[...] = last


def _conv(cu3, gb3, buf, w):
    nseq, seq, ch = cu3.shape
    rows = min(512, seq)
    sb = SUBLANES * 2 if seq <= SUBLANES else 1
    sb = min(sb, nseq)
    blk = pl.BlockSpec((sb, rows, ch), lambda b, t: (b, t, 0))
    bufspec = pl.BlockSpec((sb, CONV_WIDTH - 1, ch), lambda b, t: (b, 0, 0))
    return pl.pallas_call(
        _conv_kernel,
        grid=(nseq // sb, seq // rows),
        in_specs=[blk, blk, bufspec, _const_spec(w.shape)],
        out_specs=[blk, bufspec],
        out_shape=[jax.ShapeDtypeStruct(cu3.shape, F32), jax.ShapeDtypeStruct(buf.shape, F32)],
        scratch_shapes=[pltpu.VMEM((sb, CONV_WIDTH - 1, ch), F32)],
        compiler_params=_params(2), name="conv",
    )(cu3, gb3, buf, w)


def _out_ln_kernel(n_parts, *refs):
    parts = refs[:n_parts]
    w_ref, x_ref, g_ref, lg_ref, lb_ref, o_ref = refs[n_parts:]
    y = None
    off = 0
    for p in parts:
        width = p.shape[1]
        d = jnp.dot(p[...].astype(BF16), w_ref[off:off + width, :], preferred_element_type=F32)
        y = d if y is None else y + d
        off += width
    z = ALPHA * x_ref[...] + (1.0 + g_ref[...]) * y
    o_ref[...] = _layer_norm(z, lg_ref[...], lb_ref[...])


def _out_ln(parts, w, x, mod, rows_per_seq, lg, lb):
    n = x.shape[0]
    tm = _row_tile(n)
    row = lambda width: pl.BlockSpec((tm, width), lambda i: (i, 0))
    return pl.pallas_call(
        functools.partial(_out_ln_kernel, len(parts)),
        grid=(n // tm,),
        in_specs=[row(p.shape[1]) for p in parts]
                 + [_const_spec(w.shape), row(D_MODEL), _mod_spec(mod, 2, tm, rows_per_seq),
                    _const_spec(lg.shape), _const_spec(lb.shape)],
        out_specs=row(D_MODEL),
        out_shape=jax.ShapeDtypeStruct((n, D_MODEL), F32),
        compiler_params=_params(1), name="out_ln",
    )(*parts, w, x, mod, lg, lb)


def _router_kernel(x_ref, sc_ref, sh_ref, wr_ref, br_ref, hx_ref, gidx_ref):
    h = x_ref[...] * (1.0 + sc_ref[...]) + sh_ref[...]
    logits = jnp.dot(h, wr_ref[...], precision=HI, preferred_element_type=F32) + br_ref[...]
    lane = lax.broadcasted_iota(I32, logits.shape, 1)
    big = ROUTER_LANES
    gl = jnp.where(lane < N_GROUPS, logits, NEG_INF)
    gmax = jnp.max(gl, axis=1, keepdims=True)
    gidx = jnp.min(jnp.where(gl == gmax, lane, big), axis=1, keepdims=True)
    gsum = jnp.sum(jnp.where(lane < N_GROUPS, jnp.exp(logits - gmax), 0.0), axis=1, keepdims=True)
    g_w = 1.0 / gsum
    in_group = ((lane >= EXPERT_LANE0) & (lane < EXPERT_LANE0 + N_GROUPS * EXPERTS_PER_GROUP)
                & (((lane - EXPERT_LANE0) // EXPERTS_PER_GROUP) == gidx))
    el = jnp.where(in_group, logits, NEG_INF)
    e1 = jnp.max(el, axis=1, keepdims=True)
    i1 = jnp.min(jnp.where(el == e1, lane, big), axis=1, keepdims=True)
    el2 = jnp.where(lane == i1, NEG_INF, el)
    e2 = jnp.max(el2, axis=1, keepdims=True)
    i2 = jnp.min(jnp.where(el2 == e2, lane, big), axis=1, keepdims=True)
    t = jnp.exp(e2 - e1)
    w1 = 1.0 / (1.0 + t)
    w2 = t / (1.0 + t)
    gates = g_w * (jnp.where(lane == i1, w1, 0.0) + jnp.where(lane == i2, w2, 0.0))
    hx_ref[:, :D_MODEL] = h
    hx_ref[:, D_MODEL:] = gates
    gidx_ref[...] = jnp.broadcast_to(gidx, gidx_ref.shape)


def _router(x, mod, rows_per_seq, wr, br):
    n = x.shape[0]
    tm = _row_tile(n)
    row = lambda width: pl.BlockSpec((tm, width), lambda i: (i, 0))
    return pl.pallas_call(
        _router_kernel,
        grid=(n // tm,),
        in_specs=[row(D_MODEL), _mod_spec(mod, 4, tm, rows_per_seq), _mod_spec(mod, 3, tm, rows_per_seq),
                  _const_spec(wr.shape), _const_spec(br.shape)],
        out_specs=[row(D_MODEL + ROUTER_LANES), row(ROUTER_LANES)],
        out_shape=[jax.ShapeDtypeStruct((n, D_MODEL + ROUTER_LANES), F32), jax.ShapeDtypeStruct((n, ROUTER_LANES), I32)],
        compiler_params=_params(1), name="router",
    )(x, mod, mod, wr, br)


def _row_copy(src, src_row, dst, dst_row, sem):
    return pltpu.make_async_copy(src.at[pl.ds(src_row, 1), :], dst.at[pl.ds(dst_row, 1), :], sem)


def _moe_kernel(src_ref, tg_ref, tr_ref, hx_hbm, w1_ref, w3_ref, w2_ref, y_hbm, hbuf, ybuf, sem_in, sem_out):
    t = pl.program_id(0)
    tm = hbuf.shape[0]
    base = t * tm
    n_valid = tr_ref[t]

    @pl.when(n_valid > 0)
    def _():
        def gather_start(r, c):
            _row_copy(hx_hbm, src_ref[base + r], hbuf, r, sem_in).start()
            return c

        def gather_wait(r, c):
            _row_copy(hx_hbm, 0, hbuf, r, sem_in).wait()
            return c

        lax.fori_loop(0, tm, gather_start, 0)
        lax.fori_loop(0, tm, gather_wait, 0)

        group = tg_ref[t]
        h = hbuf[:, :D_MODEL].astype(BF16)
        gates = hbuf[:, D_MODEL:]
        lane = lax.broadcasted_iota(I32, gates.shape, 1)
        acc = jnp.zeros((tm, D_MODEL), F32)
        for e in range(EXPERTS_PER_GROUP):
            a = jnp.dot(h, w1_ref[e], preferred_element_type=F32)
            b = jnp.dot(h, w3_ref[e], preferred_element_type=F32)
            hid = (_silu(a) * b).astype(BF16)
            ge = jnp.sum(jnp.where(lane == EXPERT_LANE0 + group * EXPERTS_PER_GROUP + e, gates, 0.0),
                         axis=1, keepdims=True)
            acc = acc + ge * jnp.dot(hid, w2_ref[e], preferred_element_type=F32)
        ybuf[...] = acc

        def scatter_start(r, c):
            _row_copy(ybuf, r, y_hbm, src_ref[base + r], sem_out).start()
            return c

        def scatter_wait(r, c):
            _row_copy(ybuf, r, y_hbm, 0, sem_out).wait()
            return c

        lax.fori_loop(0, n_valid, scatter_start, 0)
        lax.fori_loop(0, n_valid, scatter_wait, 0)


def _moe(hx, src, tile_group, tile_rows, w1, w3, w2, layer):
    n = hx.shape[0]
    n_tiles = tile_group.shape[0]
    e = EXPERTS_PER_GROUP
    gs = pltpu.PrefetchScalarGridSpec(
        num_scalar_prefetch=3, grid=(n_tiles,),
        in_specs=[pl.BlockSpec(memory_space=pl.ANY),
                  pl.BlockSpec((None, None, e, D_MODEL, D_EXPERT), lambda t, s, g, r: (layer, g[t], 0, 0, 0)),
                  pl.BlockSpec((None, None, e, D_MODEL, D_EXPERT), lambda t, s, g, r: (layer, g[t], 0, 0, 0)),
                  pl.BlockSpec((None, None, e, D_EXPERT, D_MODEL), lambda t, s, g, r: (layer, g[t], 0, 0, 0))],
        out_specs=pl.BlockSpec(memory_space=pl.ANY),
        scratch_shapes=[pltpu.VMEM((MOE_TILE, hx.shape[1]), F32), pltpu.VMEM((MOE_TILE, D_MODEL), F32),
                        pltpu.SemaphoreType.DMA(()), pltpu.SemaphoreType.DMA(())])
    return pl.pallas_call(
        _moe_kernel, grid_spec=gs,
        out_shape=jax.ShapeDtypeStruct((n, D_MODEL), F32),
        compiler_params=_params(1), name="moe_experts",
    )(src, tile_group, tile_rows, hx, w1, w3, w2)


def _route_tables(gidx, n):
    tm = MOE_TILE
    n_slots = n + N_GROUPS * tm
    n_tiles = n_slots // tm
    onehot = (gidx[:, None] == jnp.arange(N_GROUPS, dtype=I32)[None, :]).astype(I32)
    counts = jnp.sum(onehot, axis=0)
    padded = ((counts + tm - 1) // tm) * tm
    ends = jnp.cumsum(padded)
    starts = ends - padded
    rank = jnp.cumsum(onehot, axis=0) - onehot
    pos = starts[gidx] + jnp.sum(rank * onehot, axis=1)
    src = jnp.zeros((n_slots,), I32).at[pos].set(jnp.arange(n, dtype=I32))
    tile_start = jnp.arange(n_tiles, dtype=I32) * tm
    tile_group = jnp.minimum(jnp.sum((tile_start[:, None] >= ends[None, :]).astype(I32), axis=1), N_GROUPS - 1)
    tile_rows = jnp.clip(starts[tile_group] + counts[tile_group] - tile_start, 0, tm).astype(I32)
    return src, tile_group.astype(I32), tile_rows


def _final_ln_kernel(x_ref, y_ref, g_ref, lg_ref, lb_ref, o_ref):
    z = ALPHA * x_ref[...] + (1.0 + g_ref[...]) * y_ref[...]
    o_ref[...] = _layer_norm(z, lg_ref[...], lb_ref[...])


def _final_ln(x, y, mod, rows_per_seq, lg, lb):
    n = x.shape[0]
    tm = _row_tile(n)
    row = pl.BlockSpec((tm, D_MODEL), lambda i: (i, 0))
    return pl.pallas_call(
        _final_ln_kernel,
        grid=(n // tm,),
        in_specs=[row, row, _mod_spec(mod, 5, tm, rows_per_seq), _const_spec(lg.shape), _const_spec(lb.shape)],
        out_specs=row,
        out_shape=jax.ShapeDtypeStruct((n, D_MODEL), F32),
        compiler_params=_params(1), name="final_ln",
    )(x, y, mod, lg, lb)


def _trunk(x3, mods, past, wts):
    batch, seq, _ = x3.shape
    n = batch * seq
    x = x3.reshape(n, D_MODEL)
    fk, fv, fl, cb, dk, dv = [], [], [], [], [], []
    if past is not None:
        c_fk, c_fv, c_fl, s_conv, c_dk, c_dv, page_table = past
        n_pages = page_table.shape[1]
        pt_flat = page_table.reshape(-1).astype(I32)
        page = c_fk.shape[2]
        rows = seq * N_HEADS
    for l in range(DEPTH):
        i = l // 2
        mod = mods[l]
        if l % 2 == 0:
            qb, kb, vb, k, v, gb, cu, lf = _even_in(x, mod, seq, wts["wm"][i], wts["wf"][i], wts["bf"][i])
            if past is None:
                logf, ctok, ct = _cumsum(lf, batch, seq)
                o_fox = _flash(qb, kb, vb, batch, seq, fox=True, extra=(ctok, ct))
                buf = jnp.zeros((batch, CONV_WIDTH - 1, CONV_DIM), F32)
            else:
                logf = lf[:, :N_HEADS]
                lnew = jnp.pad(logf.reshape(batch, 1, rows), ((0, 0), (0, 0), (0, LANES - rows)))
                to_rows = lambda a: a.reshape(batch, rows, HEAD_DIM)
                logf_pages = c_fl.reshape(c_fl.shape[0], c_fl.shape[1], 1, page * N_HEADS)
                o_fox = _fox_decode(i, pt_flat, n_pages, to_rows(qb), to_rows(kb), to_rows(vb), lnew,
                                    c_fk, c_fv, logf_pages).reshape(n, FOX_DIM)
                buf = s_conv[i]
            o_conv, new_buf = _conv(cu.reshape(batch, seq, CONV_DIM), gb.reshape(batch, seq, CONV_DIM), buf,
                                    wts["conv_w"][i])
            x = _out_ln([o_fox, o_conv.reshape(n, CONV_DIM)], wts["w_out_even"][i], x, mod, seq,
                        wts["ln_g"][l][0], wts["ln_b"][l][0])
            fk.append(k.reshape(batch, seq, N_HEADS, HEAD_DIM))
            fv.append(v.reshape(batch, seq, N_HEADS, HEAD_DIM))
            fl.append(logf.reshape(batch, seq, N_HEADS))
            cb.append(new_buf)
        else:
            lam_init = 0.8 - 0.6 * math.exp(-0.3 * l)
            qb, kb, vb, k, v = _odd_in(x, mod, seq, wts["w_qkv"][i])
            lam_vecs = [wts[name][i] for name in ("lq1", "lk1", "lq2", "lk2")]
            if past is None:
                o = _flash(qb, kb, vb, batch, seq, fox=False, extra=lam_vecs + [wts["subln"][i]], lam_init=lam_init)
            else:
                halves = lambda a: a.reshape(batch, seq, N_HEADS, 2, HEAD_DIM).transpose(0, 3, 1, 2, 4).reshape(
                    batch, 2, rows, HEAD_DIM)
                o = _diff_decode(i, pt_flat, n_pages, halves(qb), halves(kb), vb.reshape(batch, rows, 2 * HEAD_DIM),
                                 c_dk, c_dv, lam_vecs, wts["subln"][i], lam_init).reshape(n, D_MODEL)
            x = _out_ln([o], wts["w_out_odd"][i], x, mod, seq, wts["ln_g"][l][0], wts["ln_b"][l][0])
            dk.append(k.reshape(batch, seq, 2 * N_HEADS, HEAD_DIM))
            dv.append(v.reshape(batch, seq, N_HEADS, 2 * HEAD_DIM))
        hx, gidx = _router(x, mod, seq, wts["wr"][l], wts["br"][l])
        src, tile_group, tile_rows = _route_tables(gidx[:, 0], n)
        y = _moe(hx, src, tile_group, tile_rows, wts["w1"], wts["w3"], wts["w2"], l)
        x = _final_ln(x, y, mod, seq, wts["ln_g"][l][1], wts["ln_b"][l][1])
    return (x.reshape(batch, seq, D_MODEL), jnp.stack(fk), jnp.stack(fv), jnp.stack(fl), jnp.stack(cb),
            jnp.stack(dk), jnp.stack(dv))


def _prepare_weights(ln_g, ln_b, w_in_even, b_forget, conv_w, w_out_even, w_qkv_odd, lambda_q1, lambda_k1,
                     lambda_q2, lambda_k2, subln_g, w_out_odd, w_gr, b_gr, w_er, b_er, w1, w3, w2):
    f0, f1 = 3 * FOX_DIM, 3 * FOX_DIM + N_HEADS
    n_exp = N_GROUPS * EXPERTS_PER_GROUP
    pad_lanes = lambda a: jnp.pad(a, [(0, 0)] * (a.ndim - 1) + [(0, LANES - a.shape[-1])])
    wr = jnp.concatenate([w_gr, w_er.transpose(0, 2, 1, 3).reshape(DEPTH, D_MODEL, n_exp)], axis=-1)
    br = jnp.concatenate([b_gr, b_er.reshape(DEPTH, n_exp)], axis=-1)
    grouped = lambda w: w.astype(BF16).reshape(DEPTH, N_GROUPS, EXPERTS_PER_GROUP, w.shape[2], w.shape[3])
    row = lambda a: a[:, None, :]
    return dict(
        ln_g=ln_g[:, :, None, :], ln_b=ln_b[:, :, None, :],
        wm=jnp.concatenate([w_in_even[:, :, :f0], w_in_even[:, :, f1:]], axis=-1).astype(BF16),
        wf=pad_lanes(w_in_even[:, :, f0:f1]).astype(BF16), bf=pad_lanes(b_forget)[:, None, :],
        conv_w=conv_w, w_out_even=w_out_even.astype(BF16), w_qkv=w_qkv_odd.astype(BF16),
        lq1=row(lambda_q1), lk1=row(lambda_k1), lq2=row(lambda_q2), lk2=row(lambda_k2), subln=row(subln_g),
        w_out_odd=w_out_odd.astype(BF16), wr=pad_lanes(wr), br=pad_lanes(br)[:, None, :],
        w1=grouped(w1), w3=grouped(w3), w2=grouped(w2))


def kernel(x_prompt, x_sample, cache_fox_k, cache_fox_v, cache_fox_logf, state_conv, cache_diff_k, cache_diff_v,
           page_table, c_prompt, c_sample, ada_w, ada_b, ln_g, ln_b, w_in_even, b_forget, conv_w, w_out_even,
           w_qkv_odd, lambda_q1, lambda_k1, lambda_q2, lambda_k2, subln_g, w_out_odd, w_gr, b_gr, w_er, b_er,
           w1, w3, w2):
    wts = _prepare_weights(ln_g, ln_b, w_in_even, b_forget, conv_w, w_out_even, w_qkv_odd, lambda_q1, lambda_k1,
                           lambda_q2, lambda_k2, subln_g, w_out_odd, w_gr, b_gr, w_er, b_er, w1, w3, w2)
    nbp, nbs = c_prompt.shape[0], c_sample.shape[0]
    n_rows = nbp + nbs
    pad = (-n_rows) % SUBLANES
    c_all = jnp.concatenate([c_prompt, c_sample, jnp.zeros((pad, D_MODEL), F32)], axis=0)
    mod_all = _adaln(c_all, ada_w, ada_b)
    mods_p = [mod_all[l, :nbp][:, None, :] for l in range(DEPTH)]
    mods_s = [jnp.repeat(mod_all[l, nbp:n_rows], x_sample.shape[1], axis=0) for l in range(DEPTH)]
    y_p, fk_p, fv_p, fl_p, cb_p, dk_p, dv_p = _trunk(x_prompt, mods_p, None, wts)
    past = (cache_fox_k, cache_fox_v, cache_fox_logf, state_conv, cache_diff_k, cache_diff_v, page_table)
    y_s, fk_s, fv_s, fl_s, cb_s, dk_s, dv_s = _trunk(x_sample, mods_s, past, wts)
    return (y_p, y_s, fk_p, fv_p, fl_p, cb_p, dk_p, dv_p, fk_s, fv_s, fl_s, cb_s, dk_s, dv_s)
```

```python
import functools
import math

import jax
import jax.numpy as jnp
from jax import lax
from jax.experimental import pallas as pl
from jax.experimental.pallas import tpu as pltpu

F32, BF16, I32 = jnp.float32, jnp.bfloat16, jnp.int32
HI = lax.Precision.HIGHEST

D_MODEL = 1024
DEPTH = 4
HEAD_DIM = 64
N_HEADS = 8
FOX_DIM = N_HEADS * HEAD_DIM
CONV_DIM = D_MODEL - FOX_DIM
CONV_WIDTH = 3
N_GROUPS = 4
EXPERTS_PER_GROUP = 8
D_EXPERT = D_MODEL // 4
ALPHA = (2 * DEPTH) ** 0.25
LN_EPS = 1e-5
NEG_INF = -1e30
QK_SCALE = HEAD_DIM ** -0.5

LANES = 128
SUBLANES = 8
VMEM_LIMIT = 56 * 1024 * 1024
ROUTER_LANES = LANES
EXPERT_LANE0 = N_GROUPS
MOE_TILE = 256

_NT = (((1,), (1,)), ((), ()))


def _params(n_grid):
    return pltpu.CompilerParams(dimension_semantics=("arbitrary",) * n_grid, vmem_limit_bytes=VMEM_LIMIT)


def _row_tile(n):
    return min(512, n)


def _mod_spec(mod, chunk, tm, rows_per_seq):
    if mod.ndim == 3:
        tiles_per_seq = rows_per_seq // tm
        return pl.BlockSpec((None, 1, D_MODEL), lambda i: (i // tiles_per_seq, 0, chunk))
    return pl.BlockSpec((tm, D_MODEL), lambda i: (i, chunk))


def _const_spec(shape):
    return pl.BlockSpec(shape, lambda *_: (0,) * len(shape))


def _layer_norm(z, g, b):
    mu = jnp.mean(z, axis=-1, keepdims=True)
    zc = z - mu
    var = jnp.mean(zc * zc, axis=-1, keepdims=True)
    return zc * lax.rsqrt(var + LN_EPS) * g + b


def _log_sigmoid(x):
    return jnp.minimum(x, 0.0) - jnp.log1p(jnp.exp(-jnp.abs(x)))


def _silu(x):
    return x * jax.nn.sigmoid(x)


def _adaln_kernel(c_ref, w_ref, b_ref, o_ref):
    a = _silu(c_ref[...]).astype(BF16)
    o_ref[...] = jnp.dot(a, w_ref[...].astype(BF16), preferred_element_type=F32) + b_ref[...]


def _adaln(c_all, ada_w, ada_b):
    rows = c_all.shape[0]
    tn = 1536
    return pl.pallas_call(
        _adaln_kernel,
        grid=(DEPTH, 6 * D_MODEL // tn),
        in_specs=[pl.BlockSpec((rows, D_MODEL), lambda l, j: (0, 0)),
                  pl.BlockSpec((None, D_MODEL, tn), lambda l, j: (l, 0, j)),
                  pl.BlockSpec((None, 1, tn), lambda l, j: (l, 0, j))],
        out_specs=pl.BlockSpec((None, rows, tn), lambda l, j: (l, 0, j)),
        out_shape=jax.ShapeDtypeStruct((DEPTH, rows, 6 * D_MODEL), F32),
        compiler_params=_params(2), name="adaln",
    )(c_all, ada_w, ada_b.reshape(DEPTH, 1, 6 * D_MODEL))


def _even_in_kernel(x_ref, sc_ref, sh_ref, wm_ref, wf_ref, bf_ref,
                    qb_ref, kb_ref, vb_ref, k_ref, v_ref, gb_ref, cu_ref, lf_ref):
    h = (x_ref[...] * (1.0 + sc_ref[...]) + sh_ref[...]).astype(BF16)

    def mm(c):
        return jnp.dot(h, wm_ref[:, c * FOX_DIM:(c + 1) * FOX_DIM], preferred_element_type=F32)

    qb_ref[...] = (mm(0) * QK_SCALE).astype(BF16)
    k = mm(1)
    k_ref[...] = k
    kb_ref[...] = k.astype(BF16)
    v = mm(2)
    v_ref[...] = v
    vb_ref[...] = v.astype(BF16)
    gb_ref[...] = mm(3)
    cu_ref[...] = mm(4) * mm(5)
    f = jnp.dot(h, wf_ref[...], preferred_element_type=F32) + bf_ref[...]
    lane = lax.broadcasted_iota(I32, f.shape, 1)
    lf_ref[...] = jnp.where(lane < N_HEADS, _log_sigmoid(f), 0.0)


def _even_in(x, mod, rows_per_seq, wm, wf, bfp):
    n = x.shape[0]
    tm = _row_tile(n)
    row = lambda w: pl.BlockSpec((tm, w), lambda i: (i, 0))
    sds = lambda w, dt: jax.ShapeDtypeStruct((n, w), dt)
    return pl.pallas_call(
        _even_in_kernel,
        grid=(n // tm,),
        in_specs=[row(D_MODEL), _mod_spec(mod, 1, tm, rows_per_seq), _mod_spec(mod, 0, tm, rows_per_seq),
                  _const_spec(wm.shape), _const_spec(wf.shape), _const_spec(bfp.shape)],
        out_specs=[row(FOX_DIM)] * 7 + [row(LANES)],
        out_shape=[sds(FOX_DIM, BF16)] * 3 + [sds(FOX_DIM, F32)] * 4 + [sds(LANES, F32)],
        compiler_params=_params(1), name="even_in",
    )(x, mod, mod, wm, wf, bfp)


def _odd_in_kernel(x_ref, sc_ref, sh_ref, w_ref, qb_ref, kb_ref, vb_ref, k_ref, v_ref):
    h = (x_ref[...] * (1.0 + sc_ref[...]) + sh_ref[...]).astype(BF16)

    def mm(c):
        return jnp.dot(h, w_ref[:, c * D_MODEL:(c + 1) * D_MODEL], preferred_element_type=F32)

    qb_ref[...] = (mm(0) * QK_SCALE).astype(BF16)
    k = mm(1)
    k_ref[...] = k
    kb_ref[...] = k.astype(BF16)
    v = mm(2)
    v_ref[...] = v
    vb_ref[...] = v.astype(BF16)


def _odd_in(x, mod, rows_per_seq, w):
    n = x.shape[0]
    tm = _row_tile(n)
    row = pl.BlockSpec((tm, D_MODEL), lambda i: (i, 0))
    sds = lambda dt: jax.ShapeDtypeStruct((n, D_MODEL), dt)
    return pl.pallas_call(
        _odd_in_kernel,
        grid=(n // tm,),
        in_specs=[row, _mod_spec(mod, 1, tm, rows_per_seq), _mod_spec(mod, 0, tm, rows_per_seq), _const_spec(w.shape)],
        out_specs=[row] * 5,
        out_shape=[sds(BF16)] * 3 + [sds(F32)] * 2,
        compiler_params=_params(1), name="odd_in",
    )(x, mod, mod, w)


def _cumsum_kernel(lf_ref, logf_ref, ctok_ref, ct_ref, carry):
    @pl.when(pl.program_id(1) == 0)
    def _():
        carry[...] = jnp.zeros_like(carry)

    lf = lf_ref[...]
    tc = lf.shape[0]
    r = lax.broadcasted_iota(I32, (tc, tc), 0)
    c = lax.broadcasted_iota(I32, (tc, tc), 1)
    tri = (c <= r).astype(F32)
    cs = jnp.dot(tri, lf, precision=HI, preferred_element_type=F32) + carry[...]
    carry[...] = cs[tc - 1:tc, :]
    logf_ref[...] = lf[:, :N_HEADS]
    ctok_ref[...] = cs
    eye = (lax.broadcasted_iota(I32, (N_HEADS, LANES), 0) == lax.broadcasted_iota(I32, (N_HEADS, LANES), 1)).astype(F32)
    ct_ref[...] = lax.dot_general(eye, cs, _NT, precision=HI, preferred_element_type=F32)


def _cumsum(lf, batch, seq):
    n = lf.shape[0]
    tc = min(512, seq)
    nt = seq // tc
    return pl.pallas_call(
        _cumsum_kernel,
        grid=(batch, nt),
        in_specs=[pl.BlockSpec((tc, LANES), lambda b, t: (b * nt + t, 0))],
        out_specs=[pl.BlockSpec((tc, N_HEADS), lambda b, t: (b * nt + t, 0)),
                   pl.BlockSpec((tc, LANES), lambda b, t: (b * nt + t, 0)),
                   pl.BlockSpec((None, N_HEADS, tc), lambda b, t: (b, 0, t))],
        out_shape=[jax.ShapeDtypeStruct((n, N_HEADS), F32), jax.ShapeDtypeStruct((n, LANES), F32),
                   jax.ShapeDtypeStruct((batch, N_HEADS, seq), F32)],
        scratch_shapes=[pltpu.VMEM((1, LANES), F32)],
        compiler_params=_params(2), name="logf_cumsum",
    )(lf)


def _online_update(state, s, v, transposed_v=False):
    m, l, acc = state
    m_new = jnp.maximum(m, jnp.max(s, axis=1, keepdims=True))
    alpha = jnp.exp(m - m_new)
    p = jnp.exp(s - m_new)
    l = alpha * l + jnp.sum(p, axis=1, keepdims=True)
    if transposed_v:
        pv = lax.dot_general(p.astype(BF16), v, _NT, preferred_element_type=F32)
    else:
        pv = jnp.dot(p.astype(BF16), v, preferred_element_type=F32)
    return m_new, l, alpha * acc + pv


def _flash_kernel(fox, lam_init, tile, *refs):
    if fox:
        q_ref, k_ref, v_ref, ctok_ref, ck_ref, o_ref = refs
    else:
        q_ref, k_ref, v_ref, lq1_ref, lk1_ref, lq2_ref, lk2_ref, g_ref, o_ref = refs
    i = pl.program_id(2)
    q = q_ref[...]
    lane = lax.broadcasted_iota(I32, q.shape, 1)
    low = lane < HEAD_DIM
    zero = jnp.zeros_like(q)
    q_a = jnp.where(low, q, zero)
    q_b = jnp.where(low, zero, q)
    if fox:
        hp = pl.program_id(1)
        ctok = ctok_ref[...]
        cq_a = jnp.sum(jnp.where(lane == 2 * hp, ctok, 0.0), axis=1, keepdims=True)
        cq_b = jnp.sum(jnp.where(lane == 2 * hp + 1, ctok, 0.0), axis=1, keepdims=True)

    def scores(j, diagonal):
        start = pl.multiple_of(j * tile, tile)
        kj = k_ref[pl.ds(start, tile), :]
        vj = v_ref[pl.ds(start, tile), :]
        s_a = lax.dot_general(q_a, kj, _NT, preferred_element_type=F32)
        s_b = lax.dot_general(q_b, kj, _NT, preferred_element_type=F32)
        if fox:
            s_a = s_a + cq_a - ck_ref[0:1, pl.ds(start, tile)]
            s_b = s_b + cq_b - ck_ref[1:2, pl.ds(start, tile)]
        if diagonal:
            visible = lax.broadcasted_iota(I32, s_a.shape, 1) <= lax.broadcasted_iota(I32, s_a.shape, 0)
            s_a = jnp.where(visible, s_a, NEG_INF)
            s_b = jnp.where(visible, s_b, NEG_INF)
        return s_a, s_b, vj

    def step(j, state, diagonal=False):
        s_a, s_b, vj = scores(j, diagonal)
        return _online_update(state[0], s_a, vj), _online_update(state[1], s_b, vj)

    tq = q.shape[0]
    init = (jnp.full((tq, 1), NEG_INF, F32), jnp.zeros((tq, 1), F32), jnp.zeros((tq, LANES), F32))
    state = lax.fori_loop(0, i, step, (init, init))
    (_, l_a, acc_a), (_, l_b, acc_b) = step(i, state, diagonal=True)
    if fox:
        o_ref[...] = jnp.where(low, acc_a / l_a, acc_b / l_b)
    else:
        lam = (jnp.exp(jnp.sum(lq1_ref[...] * lk1_ref[...], axis=1, keepdims=True))
               - jnp.exp(jnp.sum(lq2_ref[...] * lk2_ref[...], axis=1, keepdims=True)) + lam_init)
        o = acc_a / l_a - lam * (acc_b / l_b)
        ms = jnp.mean(o * o, axis=1, keepdims=True)
        o_ref[...] = o * lax.rsqrt(ms + LN_EPS) * g_ref[...] * (1.0 - lam_init)


def _flash(qb, kb, vb, batch, seq, *, fox, extra, lam_init=0.0):
    n, width = qb.shape
    tile = min(512, seq)
    nq = seq // tile
    blocks = width // LANES
    qspec = pl.BlockSpec((tile, LANES), lambda b, h, i: (b * nq + i, h))
    kvspec = pl.BlockSpec((seq, LANES), lambda b, h, i: (b, h))
    if fox:
        ctok, ct = extra
        especs = [pl.BlockSpec((tile, LANES), lambda b, h, i: (b * nq + i, 0)),
                  pl.BlockSpec((None, None, 2, seq), lambda b, h, i: (b, h, 0, 0))]
        eargs = [ctok, ct.reshape(batch, blocks, 2, seq)]
    else:
        especs = [_const_spec(a.shape) for a in extra]
        eargs = list(extra)
    return pl.pallas_call(
        functools.partial(_flash_kernel, fox, lam_init, tile),
        grid=(batch, blocks, nq),
        in_specs=[qspec, kvspec, kvspec] + especs,
        out_specs=qspec,
        out_shape=jax.ShapeDtypeStruct((n, width), F32),
        compiler_params=_params(3), name="flash_fox" if fox else "flash_diff",
    )(qb, kb, vb, *eargs)


PAGES_PER_STEP = 8


def _pages_per_step(n_pages):
    pps = PAGES_PER_STEP
    while n_pages % pps:
        pps //= 2
    return pps


def _block_diag_queries(q, n_blocks):
    t, width = q.shape
    rows = n_blocks * t
    qf = jnp.broadcast_to(q.astype(F32)[None], (n_blocks, t, width)).reshape(rows, width)
    r = lax.broadcasted_iota(I32, (rows, width), 0)
    c = lax.broadcasted_iota(I32, (rows, width), 1)
    return jnp.where(c // HEAD_DIM == r // t, qf, 0.0).astype(BF16)


def _pad_rows(a, rows):
    return jnp.concatenate([a, jnp.zeros((rows - a.shape[0], a.shape[1]), a.dtype)], axis=0)


def _page_index(n_pages, pps, j):
    n_groups = n_pages // pps

    def index(b, s, pt):
        group = jnp.clip(n_groups - s, 0, n_groups - 1)
        return pt[b * n_pages + group * pps + j]

    return index


def _fox_dec_kernel(n_steps, pps, pt_ref, q_ref, kn_ref, vn_ref, lnt_ref, *refs):
    kc_refs, vc_refs, lp_refs = refs[:pps], refs[pps:2 * pps], refs[2 * pps:3 * pps]
    o_ref, qbd_sc, m_sc, l_sc, acc_sc, suf_sc, cq_sc = refs[3 * pps:]
    s = pl.program_id(1)
    t_new = q_ref.shape[0]
    rows = N_HEADS * t_new
    page = lp_refs[0].shape[1]

    @pl.when(s == 0)
    def _():
        qbd = _block_diag_queries(q_ref[...], N_HEADS)
        qbd_sc[...] = qbd
        x = lnt_ref[...]
        lane = lax.broadcasted_iota(I32, x.shape, 1)
        sh = 1
        while sh < t_new:
            x = x + jnp.where(lane >= sh, pltpu.roll(x, sh, axis=1), 0.0)
            sh *= 2
        ck = jnp.broadcast_to(x[:, None, :], (N_HEADS, t_new, page)).reshape(rows, page)
        t = lax.broadcasted_iota(I32, (rows, page), 1)
        qt = lax.broadcasted_iota(I32, (rows, page), 0) % t_new
        cq = jnp.sum(jnp.where(t == qt, ck, 0.0), axis=1, keepdims=True)
        cq_sc[...] = cq
        kn = _pad_rows(kn_ref[...], page).astype(BF16)
        vn = _pad_rows(vn_ref[...], page).astype(BF16)
        sc = lax.dot_general(qbd, kn, _NT, preferred_element_type=F32) + cq - ck
        sc = jnp.where(t <= qt, sc, NEG_INF)
        m = jnp.max(sc, axis=1, keepdims=True)
        p = jnp.exp(sc - m)
        m_sc[...] = m
        l_sc[...] = jnp.sum(p, axis=1, keepdims=True)
        acc_sc[...] = jnp.dot(p.astype(BF16), vn, preferred_element_type=F32)
        suf_sc[...] = jnp.zeros_like(suf_sc)

    @pl.when(s > 0)
    def _():
        qbd = qbd_sc[...]
        cq = cq_sc[...]
        state = (m_sc[...], l_sc[...], acc_sc[...])
        suf = suf_sc[...]
        lane = lax.broadcasted_iota(I32, suf.shape, 1)
        for j in reversed(range(pps)):
            lp = lp_refs[j][...]
            x = lp
            sh = 1
            while sh < page:
                x = x + jnp.where(lane + sh < page, pltpu.roll(x, page - sh, axis=1), 0.0)
                sh *= 2
            later = suf + (x - lp)
            suf = suf + jnp.sum(lp, axis=1, keepdims=True)
            bias = jnp.broadcast_to(later[:, None, :], (N_HEADS, t_new, page)).reshape(rows, page)
            kt = kc_refs[j][...].reshape(FOX_DIM, page).astype(BF16)
            vt = vc_refs[j][...].reshape(FOX_DIM, page).astype(BF16)
            sc = jnp.dot(qbd, kt, preferred_element_type=F32) + cq + bias
            state = _online_update(state, sc, vt, transposed_v=True)
        m_sc[...], l_sc[...], acc_sc[...] = state
        suf_sc[...] = suf

    @pl.when(s == n_steps - 1)
    def _():
        o = acc_sc[...] / l_sc[...]
        col_head = lax.broadcasted_iota(I32, o_ref.shape, 1) // HEAD_DIM
        out = jnp.zeros(o_ref.shape, F32)
        for h in range(N_HEADS):
            out = jnp.where(col_head == h, o[h * t_new:(h + 1) * t_new, :], out)
        o_ref[...] = out


def _fox_decode(layer, pt_flat, n_pages, q3, kn3, vn3, lnt, cache_kt, cache_vt, cache_lt):
    nb, t_new, width = q3.shape
    page = cache_kt.shape[-1]
    pps = _pages_per_step(n_pages)
    n_steps = n_pages // pps + 1
    rows = N_HEADS * t_new
    seq_spec = lambda r, w: pl.BlockSpec((None, r, w), lambda b, s, pt: (b, 0, 0))

    def cache_spec(j):
        idx = _page_index(n_pages, pps, j)
        return pl.BlockSpec((None, None, N_HEADS, HEAD_DIM, page), lambda b, s, pt: (layer, idx(b, s, pt), 0, 0, 0))

    def logf_spec(j):
        idx = _page_index(n_pages, pps, j)
        return pl.BlockSpec((None, None, N_HEADS, page), lambda b, s, pt: (layer, idx(b, s, pt), 0, 0))

    gs = pltpu.PrefetchScalarGridSpec(
        num_scalar_prefetch=1, grid=(nb, n_steps),
        in_specs=[seq_spec(t_new, width)] * 3 + [seq_spec(N_HEADS, LANES)]
                 + [cache_spec(j) for j in range(pps)] * 2 + [logf_spec(j) for j in range(pps)],
        out_specs=seq_spec(t_new, width),
        scratch_shapes=[pltpu.VMEM((rows, width), BF16), pltpu.VMEM((rows, 1), F32), pltpu.VMEM((rows, 1), F32),
                        pltpu.VMEM((rows, width), F32), pltpu.VMEM((N_HEADS, page), F32), pltpu.VMEM((rows, 1), F32)])
    return pl.pallas_call(
        functools.partial(_fox_dec_kernel, n_steps, pps), grid_spec=gs,
        out_shape=jax.ShapeDtypeStruct((nb, t_new, width), F32),
        compiler_params=_params(2), name="fox_decode",
    )(pt_flat, q3, kn3, vn3, lnt, *([cache_kt] * pps), *([cache_vt] * pps), *([cache_lt] * pps))


def _diff_dec_kernel(n_steps, pps, lam_init, pt_ref, q_ref, kn_ref, vn_ref, *refs):
    kc_refs, vc_refs = refs[:pps], refs[pps:2 * pps]
    lq1_ref, lk1_ref, lq2_ref, lk2_ref, g_ref, o_ref, qbd_sc, m_sc, l_sc, acc_sc = refs[2 * pps:]
    s = pl.program_id(1)
    t_new = q_ref.shape[0]
    n_streams = 2 * N_HEADS
    page = vc_refs[0].shape[0]
    dv = 2 * HEAD_DIM
    head_rows = 2 * t_new

    def weighted_values(p, value_of_head):
        pb = p.astype(BF16)
        return jnp.concatenate(
            [jnp.dot(pb[h * head_rows:(h + 1) * head_rows], value_of_head(h), preferred_element_type=F32)
             for h in range(N_HEADS)], axis=0)

    @pl.when(s == 0)
    def _():
        qbd = _block_diag_queries(q_ref[...], n_streams)
        qbd_sc[...] = qbd
        kn = _pad_rows(kn_ref[...], page).astype(BF16)
        vn = _pad_rows(vn_ref[...], page).astype(BF16)
        sc = lax.dot_general(qbd, kn, _NT, preferred_element_type=F32)
        t = lax.broadcasted_iota(I32, sc.shape, 1)
        qt = lax.broadcasted_iota(I32, sc.shape, 0) % t_new
        sc = jnp.where(t <= qt, sc, NEG_INF)
        m = jnp.max(sc, axis=1, keepdims=True)
        p = jnp.exp(sc - m)
        m_sc[...] = m
        l_sc[...] = jnp.sum(p, axis=1, keepdims=True)
        acc_sc[...] = weighted_values(p, lambda h: vn[:, h * dv:(h + 1) * dv])

    @pl.when(s > 0)
    def _():
        qbd = qbd_sc[...]
        m, l, acc = m_sc[...], l_sc[...], acc_sc[...]
        for j in range(pps):
            kt = kc_refs[j][...].reshape(D_MODEL, page).astype(BF16)
            sc = jnp.dot(qbd, kt, preferred_element_type=F32)
            m_new = jnp.maximum(m, jnp.max(sc, axis=1, keepdims=True))
            alpha = jnp.exp(m - m_new)
            p = jnp.exp(sc - m_new)
            l = alpha * l + jnp.sum(p, axis=1, keepdims=True)
            acc = alpha * acc + weighted_values(p, lambda h: vc_refs[j][:, h, :].astype(BF16))
            m = m_new
        m_sc[...], l_sc[...], acc_sc[...] = m, l, acc

    @pl.when(s == n_steps - 1)
    def _():
        lam = (jnp.exp(jnp.sum(lq1_ref[...] * lk1_ref[...], axis=1, keepdims=True))
               - jnp.exp(jnp.sum(lq2_ref[...] * lk2_ref[...], axis=1, keepdims=True)) + lam_init)
        a = (acc_sc[...] / l_sc[...]).reshape(N_HEADS, 2, t_new, dv)
        o = a[:, 0] - lam * a[:, 1]
        ms = jnp.mean(o * o, axis=-1, keepdims=True)
        o = o * lax.rsqrt(ms + LN_EPS) * g_ref[...] * (1.0 - lam_init)
        for h in range(N_HEADS):
            o_ref[:, h * dv:(h + 1) * dv] = o[h]


def _diff_decode(layer, pt_flat, n_pages, q3, kn3, vn3, cache_kt, cache_v, lam_vecs, subln, lam_init):
    nb, t_new, width = q3.shape
    page = cache_kt.shape[-1]
    pps = _pages_per_step(n_pages)
    n_steps = n_pages // pps + 1
    rows = 2 * N_HEADS * t_new
    dv = 2 * HEAD_DIM
    seq_spec = pl.BlockSpec((None, t_new, width), lambda b, s, pt: (b, 0, 0))

    def k_spec(j):
        idx = _page_index(n_pages, pps, j)
        return pl.BlockSpec((None, None, 2 * N_HEADS, HEAD_DIM, page), lambda b, s, pt: (layer, idx(b, s, pt), 0, 0, 0))

    def v_spec(j):
        idx = _page_index(n_pages, pps, j)
        return pl.BlockSpec((None, None, page, N_HEADS, dv), lambda b, s, pt: (layer, idx(b, s, pt), 0, 0, 0))

    gs = pltpu.PrefetchScalarGridSpec(
        num_scalar_prefetch=1, grid=(nb, n_steps),
        in_specs=[seq_spec] * 3 + [k_spec(j) for j in range(pps)] + [v_spec(j) for j in range(pps)]
                 + [_const_spec(a.shape) for a in lam_vecs] + [_const_spec(subln.shape)],
        out_specs=seq_spec,
        scratch_shapes=[pltpu.VMEM((rows, width), BF16), pltpu.VMEM((rows, 1), F32), pltpu.VMEM((rows, 1), F32),
                        pltpu.VMEM((rows, dv), F32)])
    return pl.pallas_call(
        functools.partial(_diff_dec_kernel, n_steps, pps, lam_init), grid_spec=gs,
        out_shape=jax.ShapeDtypeStruct((nb, t_new, width), F32),
        compiler_params=_params(2), name="diff_decode",
    )(pt_flat, q3, kn3, vn3, *([cache_kt] * pps), *([cache_v] * pps), *lam_vecs, subln)


def _conv_kernel(cu_ref, gb_ref, buf_ref, w_ref, o_ref, nb_ref, carry):
    @pl.when(pl.program_id(1) == 0)
    def _():
        carry[...] = buf_ref[...]

    cu = cu_ref[...]
    rows = cu.shape[1]
    t = lax.broadcasted_iota(I32, cu.shape, 1)
    prev1 = carry[:, 1:2, :]
    prev2 = carry[:, 0:1, :]
    back1 = jnp.where(t == 0, prev1, pltpu.roll(cu, 1, axis=1))
    back2 = jnp.where(t == 0, prev2, jnp.where(t == 1, prev1, pltpu.roll(cu, 2, axis=1)))
    y = w_ref[0:1, :] * back2 + w_ref[1:2, :] * back1 + w_ref[2:3, :] * cu
    o_ref[...] = gb_ref[...] * y
    last = cu[:, rows - (CONV_WIDTH - 1):, :]
    carry[...] = last
    nb_ref[...] = last


def _conv(cu3, gb3, buf, w):
    nseq, seq, ch = cu3.shape
    rows = min(512, seq)
    sb = SUBLANES * 2 if seq <= SUBLANES else 1
    sb = min(sb, nseq)
    blk = pl.BlockSpec((sb, rows, ch), lambda b, t: (b, t, 0))
    bufspec = pl.BlockSpec((sb, CONV_WIDTH - 1, ch), lambda b, t: (b, 0, 0))
    return pl.pallas_call(
        _conv_kernel,
        grid=(nseq // sb, seq // rows),
        in_specs=[blk, blk, bufspec, _const_spec(w.shape)],
        out_specs=[blk, bufspec],
        out_shape=[jax.ShapeDtypeStruct(cu3.shape, F32), jax.ShapeDtypeStruct(buf.shape, F32)],
        scratch_shapes=[pltpu.VMEM((sb, CONV_WIDTH - 1, ch), F32)],
        compiler_params=_params(2), name="conv",
    )(cu3, gb3, buf, w)


def _out_ln_kernel(n_parts, *refs):
    parts = refs[:n_parts]
    w_ref, x_ref, g_ref, lg_ref, lb_ref, o_ref = refs[n_parts:]
    y = None
    off = 0
    for p in parts:
        width = p.shape[1]
        d = jnp.dot(p[...].astype(BF16), w_ref[off:off + width, :], preferred_element_type=F32)
        y = d if y is None else y + d
        off += width
    z = ALPHA * x_ref[...] + (1.0 + g_ref[...]) * y
    o_ref[...] = _layer_norm(z, lg_ref[...], lb_ref[...])


def _out_ln(parts, w, x, mod, rows_per_seq, lg, lb):
    n = x.shape[0]
    tm = _row_tile(n)
    row = lambda width: pl.BlockSpec((tm, width), lambda i: (i, 0))
    return pl.pallas_call(
        functools.partial(_out_ln_kernel, len(parts)),
        grid=(n // tm,),
        in_specs=[row(p.shape[1]) for p in parts]
                 + [_const_spec(w.shape), row(D_MODEL), _mod_spec(mod, 2, tm, rows_per_seq),
                    _const_spec(lg.shape), _const_spec(lb.shape)],
        out_specs=row(D_MODEL),
        out_shape=jax.ShapeDtypeStruct((n, D_MODEL), F32),
        compiler_params=_params(1), name="out_ln",
    )(*parts, w, x, mod, lg, lb)


def _router_kernel(x_ref, sc_ref, sh_ref, wr_ref, br_ref, hx_ref, gidx_ref):
    h = x_ref[...] * (1.0 + sc_ref[...]) + sh_ref[...]
    logits = jnp.dot(h, wr_ref[...], precision=HI, preferred_element_type=F32) + br_ref[...]
    lane = lax.broadcasted_iota(I32, logits.shape, 1)
    big = ROUTER_LANES
    gl = jnp.where(lane < N_GROUPS, logits, NEG_INF)
    gmax = jnp.max(gl, axis=1, keepdims=True)
    gidx = jnp.min(jnp.where(gl == gmax, lane, big), axis=1, keepdims=True)
    gsum = jnp.sum(jnp.where(lane < N_GROUPS, jnp.exp(logits - gmax), 0.0), axis=1, keepdims=True)
    g_w = 1.0 / gsum
    in_group = ((lane >= EXPERT_LANE0) & (lane < EXPERT_LANE0 + N_GROUPS * EXPERTS_PER_GROUP)
                & (((lane - EXPERT_LANE0) // EXPERTS_PER_GROUP) == gidx))
    el = jnp.where(in_group, logits, NEG_INF)
    e1 = jnp.max(el, axis=1, keepdims=True)
    i1 = jnp.min(jnp.where(el == e1, lane, big), axis=1, keepdims=True)
    el2 = jnp.where(lane == i1, NEG_INF, el)
    e2 = jnp.max(el2, axis=1, keepdims=True)
    i2 = jnp.min(jnp.where(el2 == e2, lane, big), axis=1, keepdims=True)
    t = jnp.exp(e2 - e1)
    w1 = 1.0 / (1.0 + t)
    w2 = t / (1.0 + t)
    gates = g_w * (jnp.where(lane == i1, w1, 0.0) + jnp.where(lane == i2, w2, 0.0))
    hx_ref[:, :D_MODEL] = h
    hx_ref[:, D_MODEL:] = gates
    gidx_ref[...] = jnp.broadcast_to(gidx, gidx_ref.shape)


def _router(x, mod, rows_per_seq, wr, br):
    n = x.shape[0]
    tm = _row_tile(n)
    row = lambda width: pl.BlockSpec((tm, width), lambda i: (i, 0))
    return pl.pallas_call(
        _router_kernel,
        grid=(n // tm,),
        in_specs=[row(D_MODEL), _mod_spec(mod, 4, tm, rows_per_seq), _mod_spec(mod, 3, tm, rows_per_seq),
                  _const_spec(wr.shape), _const_spec(br.shape)],
        out_specs=[row(D_MODEL + ROUTER_LANES), row(ROUTER_LANES)],
        out_shape=[jax.ShapeDtypeStruct((n, D_MODEL + ROUTER_LANES), F32), jax.ShapeDtypeStruct((n, ROUTER_LANES), I32)],
        compiler_params=_params(1), name="router",
    )(x, mod, mod, wr, br)


def _row_copy(src, src_row, dst, dst_row, sem):
    return pltpu.make_async_copy(src.at[pl.ds(src_row, 1), :], dst.at[pl.ds(dst_row, 1), :], sem)


def _moe_kernel(src_ref, tg_ref, tr_ref, hx_hbm, w1_ref, w3_ref, w2_ref, y_hbm, hbuf, ybuf, sem_in, sem_out):
    t = pl.program_id(0)
    tm = hbuf.shape[0]
    base = t * tm
    n_valid = tr_ref[t]

    @pl.when(n_valid > 0)
    def _():
        def gather_start(r, c):
            _row_copy(hx_hbm, src_ref[base + r], hbuf, r, sem_in).start()
            return c

        def gather_wait(r, c):
            _row_copy(hx_hbm, 0, hbuf, r, sem_in).wait()
            return c

        lax.fori_loop(0, tm, gather_start, 0)
        lax.fori_loop(0, tm, gather_wait, 0)

        group = tg_ref[t]
        h = hbuf[:, :D_MODEL].astype(BF16)
        gates = hbuf[:, D_MODEL:]
        lane = lax.broadcasted_iota(I32, gates.shape, 1)
        acc = jnp.zeros((tm, D_MODEL), F32)
        for e in range(EXPERTS_PER_GROUP):
            a = jnp.dot(h, w1_ref[e], preferred_element_type=F32)
            b = jnp.dot(h, w3_ref[e], preferred_element_type=F32)
            hid = (_silu(a) * b).astype(BF16)
            ge = jnp.sum(jnp.where(lane == EXPERT_LANE0 + group * EXPERTS_PER_GROUP + e, gates, 0.0),
                         axis=1, keepdims=True)
            acc = acc + ge * jnp.dot(hid, w2_ref[e], preferred_element_type=F32)
        ybuf[...] = acc

        def scatter_start(r, c):
            _row_copy(ybuf, r, y_hbm, src_ref[base + r], sem_out).start()
            return c

        def scatter_wait(r, c):
            _row_copy(ybuf, r, y_hbm, 0, sem_out).wait()
            return c

        lax.fori_loop(0, n_valid, scatter_start, 0)
        lax.fori_loop(0, n_valid, scatter_wait, 0)


def _moe(hx, src, tile_group, tile_rows, w1, w3, w2, layer):
    n = hx.shape[0]
    n_tiles = tile_group.shape[0]
    e = EXPERTS_PER_GROUP
    gs = pltpu.PrefetchScalarGridSpec(
        num_scalar_prefetch=3, grid=(n_tiles,),
        in_specs=[pl.BlockSpec(memory_space=pl.ANY),
                  pl.BlockSpec((None, None, e, D_MODEL, D_EXPERT), lambda t, s, g, r: (layer, g[t], 0, 0, 0)),
                  pl.BlockSpec((None, None, e, D_MODEL, D_EXPERT), lambda t, s, g, r: (layer, g[t], 0, 0, 0)),
                  pl.BlockSpec((None, None, e, D_EXPERT, D_MODEL), lambda t, s, g, r: (layer, g[t], 0, 0, 0))],
        out_specs=pl.BlockSpec(memory_space=pl.ANY),
        scratch_shapes=[pltpu.VMEM((MOE_TILE, hx.shape[1]), F32), pltpu.VMEM((MOE_TILE, D_MODEL), F32),
                        pltpu.SemaphoreType.DMA(()), pltpu.SemaphoreType.DMA(())])
    return pl.pallas_call(
        _moe_kernel, grid_spec=gs,
        out_shape=jax.ShapeDtypeStruct((n, D_MODEL), F32),
        compiler_params=_params(1), name="moe_experts",
    )(src, tile_group, tile_rows, hx, w1, w3, w2)


def _route_tables(gidx, n):
    tm = MOE_TILE
    n_slots = n + N_GROUPS * tm
    n_tiles = n_slots // tm
    onehot = (gidx[:, None] == jnp.arange(N_GROUPS, dtype=I32)[None, :]).astype(I32)
    counts = jnp.sum(onehot, axis=0)
    padded = ((counts + tm - 1) // tm) * tm
    ends = jnp.cumsum(padded)
    starts = ends - padded
    rank = jnp.cumsum(onehot, axis=0) - onehot
    pos = starts[gidx] + jnp.sum(rank * onehot, axis=1)
    src = jnp.zeros((n_slots,), I32).at[pos].set(jnp.arange(n, dtype=I32))
    tile_start = jnp.arange(n_tiles, dtype=I32) * tm
    tile_group = jnp.minimum(jnp.sum((tile_start[:, None] >= ends[None, :]).astype(I32), axis=1), N_GROUPS - 1)
    tile_rows = jnp.clip(starts[tile_group] + counts[tile_group] - tile_start, 0, tm).astype(I32)
    return src, tile_group.astype(I32), tile_rows


def _final_ln_kernel(x_ref, y_ref, g_ref, lg_ref, lb_ref, o_ref):
    z = ALPHA * x_ref[...] + (1.0 + g_ref[...]) * y_ref[...]
    o_ref[...] = _layer_norm(z, lg_ref[...], lb_ref[...])


def _final_ln(x, y, mod, rows_per_seq, lg, lb):
    n = x.shape[0]
    tm = _row_tile(n)
    row = pl.BlockSpec((tm, D_MODEL), lambda i: (i, 0))
    return pl.pallas_call(
        _final_ln_kernel,
        grid=(n // tm,),
        in_specs=[row, row, _mod_spec(mod, 5, tm, rows_per_seq), _const_spec(lg.shape), _const_spec(lb.shape)],
        out_specs=row,
        out_shape=jax.ShapeDtypeStruct((n, D_MODEL), F32),
        compiler_params=_params(1), name="final_ln",
    )(x, y, mod, lg, lb)


def _trunk(x3, mods, past, wts):
    batch, seq, _ = x3.shape
    n = batch * seq
    x = x3.reshape(n, D_MODEL)
    fk, fv, fl, cb, dk, dv = [], [], [], [], [], []
    if past is not None:
        c_fk, c_fv, c_fl, s_conv, c_dk, c_dv, page_table = past
        n_pages = page_table.shape[1]
        pt_flat = page_table.reshape(-1).astype(I32)
        fk_t = jnp.transpose(c_fk, (0, 1, 3, 4, 2))
        fv_t = jnp.transpose(c_fv, (0, 1, 3, 4, 2))
        fl_t = jnp.transpose(c_fl, (0, 1, 3, 2))
        dk_t = jnp.transpose(c_dk, (0, 1, 3, 4, 2))
        per_seq = lambda a: a.reshape(batch, seq, a.shape[-1])
    for l in range(DEPTH):
        i = l // 2
        mod = mods[l]
        if l % 2 == 0:
            qb, kb, vb, k, v, gb, cu, lf = _even_in(x, mod, seq, wts["wm"][i], wts["wf"][i], wts["bf"][i])
            if past is None:
                logf, ctok, ct = _cumsum(lf, batch, seq)
                o_fox = _flash(qb, kb, vb, batch, seq, fox=True, extra=(ctok, ct))
                buf = jnp.zeros((batch, CONV_WIDTH - 1, CONV_DIM), F32)
            else:
                logf = lf[:, :N_HEADS]
                lnt = jnp.pad(jnp.transpose(per_seq(logf), (0, 2, 1)), ((0, 0), (0, 0), (0, LANES - seq)))
                o_fox = _fox_decode(i, pt_flat, n_pages, per_seq(qb), per_seq(k), per_seq(v), lnt,
                                    fk_t, fv_t, fl_t).reshape(n, FOX_DIM)
                buf = s_conv[i]
            o_conv, new_buf = _conv(cu.reshape(batch, seq, CONV_DIM), gb.reshape(batch, seq, CONV_DIM), buf,
                                    wts["conv_w"][i])
            x = _out_ln([o_fox, o_conv.reshape(n, CONV_DIM)], wts["w_out_even"][i], x, mod, seq,
                        wts["ln_g"][l][0], wts["ln_b"][l][0])
            fk.append(k.reshape(batch, seq, N_HEADS, HEAD_DIM))
            fv.append(v.reshape(batch, seq, N_HEADS, HEAD_DIM))
            fl.append(logf.reshape(batch, seq, N_HEADS))
            cb.append(new_buf)
        else:
            lam_init = 0.8 - 0.6 * math.exp(-0.3 * l)
            qb, kb, vb, k, v = _odd_in(x, mod, seq, wts["w_qkv"][i])
            lam_vecs = [wts[name][i] for name in ("lq1", "lk1", "lq2", "lk2")]
            if past is None:
                o = _flash(qb, kb, vb, batch, seq, fox=False, extra=lam_vecs + [wts["subln"][i]], lam_init=lam_init)
            else:
                o = _diff_decode(i, pt_flat, n_pages, per_seq(qb), per_seq(k), per_seq(v), dk_t, c_dv,
                                 lam_vecs, wts["subln"][i], lam_init).reshape(n, D_MODEL)
            x = _out_ln([o], wts["w_out_odd"][i], x, mod, seq, wts["ln_g"][l][0], wts["ln_b"][l][0])
            dk.append(k.reshape(batch, seq, 2 * N_HEADS, HEAD_DIM))
            dv.append(v.reshape(batch, seq, N_HEADS, 2 * HEAD_DIM))
        hx, gidx = _router(x, mod, seq, wts["wr"][l], wts["br"][l])
        src, tile_group, tile_rows = _route_tables(gidx[:, 0], n)
        y = _moe(hx, src, tile_group, tile_rows, wts["w1"], wts["w3"], wts["w2"], l)
        x = _final_ln(x, y, mod, seq, wts["ln_g"][l][1], wts["ln_b"][l][1])
    return (x.reshape(batch, seq, D_MODEL), jnp.stack(fk), jnp.stack(fv), jnp.stack(fl), jnp.stack(cb),
            jnp.stack(dk), jnp.stack(dv))


def _prepare_weights(ln_g, ln_b, w_in_even, b_forget, conv_w, w_out_even, w_qkv_odd, lambda_q1, lambda_k1,
                     lambda_q2, lambda_k2, subln_g, w_out_odd, w_gr, b_gr, w_er, b_er, w1, w3, w2):
    f0, f1 = 3 * FOX_DIM, 3 * FOX_DIM + N_HEADS
    n_exp = N_GROUPS * EXPERTS_PER_GROUP
    pad_lanes = lambda a: jnp.pad(a, [(0, 0)] * (a.ndim - 1) + [(0, LANES - a.shape[-1])])
    wr = jnp.concatenate([w_gr, w_er.transpose(0, 2, 1, 3).reshape(DEPTH, D_MODEL, n_exp)], axis=-1)
    br = jnp.concatenate([b_gr, b_er.reshape(DEPTH, n_exp)], axis=-1)
    grouped = lambda w: w.astype(BF16).reshape(DEPTH, N_GROUPS, EXPERTS_PER_GROUP, w.shape[2], w.shape[3])
    row = lambda a: a[:, None, :]
    return dict(
        ln_g=ln_g[:, :, None, :], ln_b=ln_b[:, :, None, :],
        wm=jnp.concatenate([w_in_even[:, :, :f0], w_in_even[:, :, f1:]], axis=-1).astype(BF16),
        wf=pad_lanes(w_in_even[:, :, f0:f1]).astype(BF16), bf=pad_lanes(b_forget)[:, None, :],
        conv_w=conv_w, w_out_even=w_out_even.astype(BF16), w_qkv=w_qkv_odd.astype(BF16),
        lq1=row(lambda_q1), lk1=row(lambda_k1), lq2=row(lambda_q2), lk2=row(lambda_k2), subln=row(subln_g),
        w_out_odd=w_out_odd.astype(BF16), wr=pad_lanes(wr), br=pad_lanes(br)[:, None, :],
        w1=grouped(w1), w3=grouped(w3), w2=grouped(w2))


def kernel(x_prompt, x_sample, cache_fox_k, cache_fox_v, cache_fox_logf, state_conv, cache_diff_k, cache_diff_v,
           page_table, c_prompt, c_sample, ada_w, ada_b, ln_g, ln_b, w_in_even, b_forget, conv_w, w_out_even,
           w_qkv_odd, lambda_q1, lambda_k1, lambda_q2, lambda_k2, subln_g, w_out_odd, w_gr, b_gr, w_er, b_er,
           w1, w3, w2):
    wts = _prepare_weights(ln_g, ln_b, w_in_even, b_forget, conv_w, w_out_even, w_qkv_odd, lambda_q1, lambda_k1,
                           lambda_q2, lambda_k2, subln_g, w_out_odd, w_gr, b_gr, w_er, b_er, w1, w3, w2)
    nbp, nbs = c_prompt.shape[0], c_sample.shape[0]
    n_rows = nbp + nbs
    pad = (-n_rows) % SUBLANES
    c_all = jnp.concatenate([c_prompt, c_sample, jnp.zeros((pad, D_MODEL), F32)], axis=0)
    mod_all = _adaln(c_all, ada_w, ada_b)
    mods_p = [mod_all[l, :nbp][:, None, :] for l in range(DEPTH)]
    mods_s = [jnp.repeat(mod_all[l, nbp:n_rows], x_sample.shape[1], axis=0) for l in range(DEPTH)]
    y_p, fk_p, fv_p, fl_p, cb_p, dk_p, dv_p = _trunk(x_prompt, mods_p, None, wts)
    past = (cache_fox_k, cache_fox_v, cache_fox_logf, state_conv, cache_diff_k, cache_diff_v, page_table)
    y_s, fk_s, fv_s, fl_s, cb_s, dk_s, dv_s = _trunk(x_sample, mods_s, past, wts)
    return (y_p, y_s, fk_p, fv_p, fl_p, cb_p, dk_p, dv_p, fk_s, fv_s, fl_s, cb_s, dk_s, dv_s)
```

```python
import functools
import math

import jax
import jax.numpy as jnp
from jax import lax
from jax.experimental import pallas as pl
from jax.experimental.pallas import tpu as pltpu

F32, BF16, I32 = jnp.float32, jnp.bfloat16, jnp.int32
HI = lax.Precision.HIGHEST

D_MODEL = 1024
DEPTH = 4
HEAD_DIM = 64
N_HEADS = 8
FOX_DIM = N_HEADS * HEAD_DIM
CONV_DIM = D_MODEL - FOX_DIM
CONV_WIDTH = 3
N_GROUPS = 4
EXPERTS_PER_GROUP = 8
D_EXPERT = D_MODEL // 4
ALPHA = (2 * DEPTH) ** 0.25
LN_EPS = 1e-5
NEG_INF = -1e30
QK_SCALE = HEAD_DIM ** -0.5

LANES = 128
SUBLANES = 8
VMEM_LIMIT = 56 * 1024 * 1024
ROUTER_LANES = LANES
EXPERT_LANE0 = N_GROUPS
MOE_TILE = 256

_NT = (((1,), (1,)), ((), ()))


def _params(n_grid):
    return pltpu.CompilerParams(dimension_semantics=("arbitrary",) * n_grid, vmem_limit_bytes=VMEM_LIMIT)


def _row_tile(n):
    return min(512, n)


def _mod_spec(mod, chunk, tm, rows_per_seq):
    if mod.ndim == 3:
        tiles_per_seq = rows_per_seq // tm
        return pl.BlockSpec((None, 1, D_MODEL), lambda i: (i // tiles_per_seq, 0, chunk))
    return pl.BlockSpec((tm, D_MODEL), lambda i: (i, chunk))


def _const_spec(shape):
    return pl.BlockSpec(shape, lambda *_: (0,) * len(shape))


def _layer_norm(z, g, b):
    mu = jnp.mean(z, axis=-1, keepdims=True)
    zc = z - mu
    var = jnp.mean(zc * zc, axis=-1, keepdims=True)
    return zc * lax.rsqrt(var + LN_EPS) * g + b


def _log_sigmoid(x):
    return jnp.minimum(x, 0.0) - jnp.log1p(jnp.exp(-jnp.abs(x)))


def _silu(x):
    return x * jax.nn.sigmoid(x)


def _adaln_kernel(c_ref, w_ref, b_ref, o_ref):
    a = _silu(c_ref[...]).astype(BF16)
    o_ref[...] = jnp.dot(a, w_ref[...].astype(BF16), preferred_element_type=F32) + b_ref[...]


def _adaln(c_all, ada_w, ada_b):
    rows = c_all.shape[0]
    tn = 1536
    return pl.pallas_call(
        _adaln_kernel,
        grid=(DEPTH, 6 * D_MODEL // tn),
        in_specs=[pl.BlockSpec((rows, D_MODEL), lambda l, j: (0, 0)),
                  pl.BlockSpec((None, D_MODEL, tn), lambda l, j: (l, 0, j)),
                  pl.BlockSpec((None, 1, tn), lambda l, j: (l, 0, j))],
        out_specs=pl.BlockSpec((None, rows, tn), lambda l, j: (l, 0, j)),
        out_shape=jax.ShapeDtypeStruct((DEPTH, rows, 6 * D_MODEL), F32),
        compiler_params=_params(2), name="adaln",
    )(c_all, ada_w, ada_b.reshape(DEPTH, 1, 6 * D_MODEL))


def _even_in_kernel(x_ref, sc_ref, sh_ref, wm_ref, wf_ref, bf_ref,
                    qb_ref, kb_ref, vb_ref, k_ref, v_ref, gb_ref, cu_ref, lf_ref):
    h = (x_ref[...] * (1.0 + sc_ref[...]) + sh_ref[...]).astype(BF16)

    def mm(c):
        return jnp.dot(h, wm_ref[:, c * FOX_DIM:(c + 1) * FOX_DIM], preferred_element_type=F32)

    qb_ref[...] = (mm(0) * QK_SCALE).astype(BF16)
    k = mm(1)
    k_ref[...] = k
    kb_ref[...] = k.astype(BF16)
    v = mm(2)
    v_ref[...] = v
    vb_ref[...] = v.astype(BF16)
    gb_ref[...] = mm(3)
    cu_ref[...] = mm(4) * mm(5)
    f = jnp.dot(h, wf_ref[...], preferred_element_type=F32) + bf_ref[...]
    lane = lax.broadcasted_iota(I32, f.shape, 1)
    lf_ref[...] = jnp.where(lane < N_HEADS, _log_sigmoid(f), 0.0)


def _even_in(x, mod, rows_per_seq, wm, wf, bfp):
    n = x.shape[0]
    tm = _row_tile(n)
    row = lambda w: pl.BlockSpec((tm, w), lambda i: (i, 0))
    sds = lambda w, dt: jax.ShapeDtypeStruct((n, w), dt)
    return pl.pallas_call(
        _even_in_kernel,
        grid=(n // tm,),
        in_specs=[row(D_MODEL), _mod_spec(mod, 1, tm, rows_per_seq), _mod_spec(mod, 0, tm, rows_per_seq),
                  _const_spec(wm.shape), _const_spec(wf.shape), _const_spec(bfp.shape)],
        out_specs=[row(FOX_DIM)] * 7 + [row(LANES)],
        out_shape=[sds(FOX_DIM, BF16)] * 3 + [sds(FOX_DIM, F32)] * 4 + [sds(LANES, F32)],
        compiler_params=_params(1), name="even_in",
    )(x, mod, mod, wm, wf, bfp)


def _odd_in_kernel(x_ref, sc_ref, sh_ref, w_ref, qb_ref, kb_ref, vb_ref, k_ref, v_ref):
    h = (x_ref[...] * (1.0 + sc_ref[...]) + sh_ref[...]).astype(BF16)

    def mm(c):
        return jnp.dot(h, w_ref[:, c * D_MODEL:(c + 1) * D_MODEL], preferred_element_type=F32)

    qb_ref[...] = (mm(0) * QK_SCALE).astype(BF16)
    k = mm(1)
    k_ref[...] = k
    kb_ref[...] = k.astype(BF16)
    v = mm(2)
    v_ref[...] = v
    vb_ref[...] = v.astype(BF16)


def _odd_in(x, mod, rows_per_seq, w):
    n = x.shape[0]
    tm = _row_tile(n)
    row = pl.BlockSpec((tm, D_MODEL), lambda i: (i, 0))
    sds = lambda dt: jax.ShapeDtypeStruct((n, D_MODEL), dt)
    return pl.pallas_call(
        _odd_in_kernel,
        grid=(n // tm,),
        in_specs=[row, _mod_spec(mod, 1, tm, rows_per_seq), _mod_spec(mod, 0, tm, rows_per_seq), _const_spec(w.shape)],
        out_specs=[row] * 5,
        out_shape=[sds(BF16)] * 3 + [sds(F32)] * 2,
        compiler_params=_params(1), name="odd_in",
    )(x, mod, mod, w)


def _cumsum_kernel(lf_ref, logf_ref, ctok_ref, ct_ref, carry):
    @pl.when(pl.program_id(1) == 0)
    def _():
        carry[...] = jnp.zeros_like(carry)

    lf = lf_ref[...]
    tc = lf.shape[0]
    r = lax.broadcasted_iota(I32, (tc, tc), 0)
    c = lax.broadcasted_iota(I32, (tc, tc), 1)
    tri = (c <= r).astype(F32)
    cs = jnp.dot(tri, lf, precision=HI, preferred_element_type=F32) + carry[...]
    carry[...] = cs[tc - 1:tc, :]
    logf_ref[...] = lf[:, :N_HEADS]
    ctok_ref[...] = cs
    eye = (lax.broadcasted_iota(I32, (N_HEADS, LANES), 0) == lax.broadcasted_iota(I32, (N_HEADS, LANES), 1)).astype(F32)
    ct_ref[...] = lax.dot_general(eye, cs, _NT, precision=HI, preferred_element_type=F32)


def _cumsum(lf, batch, seq):
    n = lf.shape[0]
    tc = min(512, seq)
    nt = seq // tc
    return pl.pallas_call(
        _cumsum_kernel,
        grid=(batch, nt),
        in_specs=[pl.BlockSpec((tc, LANES), lambda b, t: (b * nt + t, 0))],
        out_specs=[pl.BlockSpec((tc, N_HEADS), lambda b, t: (b * nt + t, 0)),
                   pl.BlockSpec((tc, LANES), lambda b, t: (b * nt + t, 0)),
                   pl.BlockSpec((None, N_HEADS, tc), lambda b, t: (b, 0, t))],
        out_shape=[jax.ShapeDtypeStruct((n, N_HEADS), F32), jax.ShapeDtypeStruct((n, LANES), F32),
                   jax.ShapeDtypeStruct((batch, N_HEADS, seq), F32)],
        scratch_shapes=[pltpu.VMEM((1, LANES), F32)],
        compiler_params=_params(2), name="logf_cumsum",
    )(lf)


def _online_update(state, s, v):
    m, l, acc = state
    m_new = jnp.maximum(m, jnp.max(s, axis=1, keepdims=True))
    alpha = jnp.exp(m - m_new)
    p = jnp.exp(s - m_new)
    l = alpha * l + jnp.sum(p, axis=1, keepdims=True)
    acc = alpha * acc + jnp.dot(p.astype(BF16), v, preferred_element_type=F32)
    return m_new, l, acc


def _flash_kernel(fox, lam_init, tile, *refs):
    if fox:
        q_ref, k_ref, v_ref, ctok_ref, ck_ref, o_ref = refs
    else:
        q_ref, k_ref, v_ref, lq1_ref, lk1_ref, lq2_ref, lk2_ref, g_ref, o_ref = refs
    i = pl.program_id(2)
    q = q_ref[...]
    lane = lax.broadcasted_iota(I32, q.shape, 1)
    low = lane < HEAD_DIM
    zero = jnp.zeros_like(q)
    q_a = jnp.where(low, q, zero)
    q_b = jnp.where(low, zero, q)
    if fox:
        hp = pl.program_id(1)
        ctok = ctok_ref[...]
        cq_a = jnp.sum(jnp.where(lane == 2 * hp, ctok, 0.0), axis=1, keepdims=True)
        cq_b = jnp.sum(jnp.where(lane == 2 * hp + 1, ctok, 0.0), axis=1, keepdims=True)

    def scores(j, diagonal):
        start = pl.multiple_of(j * tile, tile)
        kj = k_ref[pl.ds(start, tile), :]
        vj = v_ref[pl.ds(start, tile), :]
        s_a = lax.dot_general(q_a, kj, _NT, preferred_element_type=F32)
        s_b = lax.dot_general(q_b, kj, _NT, preferred_element_type=F32)
        if fox:
            s_a = s_a + cq_a - ck_ref[0:1, pl.ds(start, tile)]
            s_b = s_b + cq_b - ck_ref[1:2, pl.ds(start, tile)]
        if diagonal:
            visible = lax.broadcasted_iota(I32, s_a.shape, 1) <= lax.broadcasted_iota(I32, s_a.shape, 0)
            s_a = jnp.where(visible, s_a, NEG_INF)
            s_b = jnp.where(visible, s_b, NEG_INF)
        return s_a, s_b, vj

    def step(j, state, diagonal=False):
        s_a, s_b, vj = scores(j, diagonal)
        return _online_update(state[0], s_a, vj), _online_update(state[1], s_b, vj)

    tq = q.shape[0]
    init = (jnp.full((tq, 1), NEG_INF, F32), jnp.zeros((tq, 1), F32), jnp.zeros((tq, LANES), F32))
    state = lax.fori_loop(0, i, step, (init, init))
    (_, l_a, acc_a), (_, l_b, acc_b) = step(i, state, diagonal=True)
    if fox:
        o_ref[...] = jnp.where(low, acc_a / l_a, acc_b / l_b)
    else:
        lam = (jnp.exp(jnp.sum(lq1_ref[...] * lk1_ref[...], axis=1, keepdims=True))
               - jnp.exp(jnp.sum(lq2_ref[...] * lk2_ref[...], axis=1, keepdims=True)) + lam_init)
        o = acc_a / l_a - lam * (acc_b / l_b)
        ms = jnp.mean(o * o, axis=1, keepdims=True)
        o_ref[...] = o * lax.rsqrt(ms + LN_EPS) * g_ref[...] * (1.0 - lam_init)


def _flash(qb, kb, vb, batch, seq, *, fox, extra, lam_init=0.0):
    n, width = qb.shape
    tile = min(512, seq)
    nq = seq // tile
    blocks = width // LANES
    qspec = pl.BlockSpec((tile, LANES), lambda b, h, i: (b * nq + i, h))
    kvspec = pl.BlockSpec((seq, LANES), lambda b, h, i: (b, h))
    if fox:
        ctok, ct = extra
        especs = [pl.BlockSpec((tile, LANES), lambda b, h, i: (b * nq + i, 0)),
                  pl.BlockSpec((None, None, 2, seq), lambda b, h, i: (b, h, 0, 0))]
        eargs = [ctok, ct.reshape(batch, blocks, 2, seq)]
    else:
        especs = [_const_spec(a.shape) for a in extra]
        eargs = list(extra)
    return pl.pallas_call(
        functools.partial(_flash_kernel, fox, lam_init, tile),
        grid=(batch, blocks, nq),
        in_specs=[qspec, kvspec, kvspec] + especs,
        out_specs=qspec,
        out_shape=jax.ShapeDtypeStruct((n, width), F32),
        compiler_params=_params(3), name="flash_fox" if fox else "flash_diff",
    )(qb, kb, vb, *eargs)


MAX_DECODE_PAGES = 16


def _block_diag_queries(q, n_blocks):
    t, width = q.shape
    rows = n_blocks * t
    qf = jnp.broadcast_to(q.astype(F32)[None], (n_blocks, t, width)).reshape(rows, width)
    r = lax.broadcasted_iota(I32, (rows, width), 0)
    c = lax.broadcasted_iota(I32, (rows, width), 1)
    return jnp.where(c // HEAD_DIM == r // t, qf, 0.0).astype(BF16)


def _pad_rows(a, rows):
    return jnp.concatenate([a, jnp.zeros((rows - a.shape[0], a.shape[1]), a.dtype)], axis=0)


def _softmax_terms(s):
    m = jnp.max(s, axis=1, keepdims=True)
    p = jnp.exp(s - m)
    return p.astype(BF16), jnp.sum(p, axis=1, keepdims=True)


def _fox_dec_kernel(n_pages, pt_ref, q_ref, kn_ref, vn_ref, lnt_ref, *refs):
    kc_refs, vc_refs, lp_refs = refs[:n_pages], refs[n_pages:2 * n_pages], refs[2 * n_pages:3 * n_pages]
    o_ref = refs[3 * n_pages]
    t_new = q_ref.shape[0]
    rows = N_HEADS * t_new
    page = lp_refs[0].shape[1]
    qbd = _block_diag_queries(q_ref[...], N_HEADS)

    x = lnt_ref[...]
    lane = lax.broadcasted_iota(I32, x.shape, 1)
    sh = 1
    while sh < t_new:
        x = x + jnp.where(lane >= sh, pltpu.roll(x, sh, axis=1), 0.0)
        sh *= 2
    ck = jnp.broadcast_to(x[:, None, :], (N_HEADS, t_new, page)).reshape(rows, page)
    t = lax.broadcasted_iota(I32, (rows, page), 1)
    qt = lax.broadcasted_iota(I32, (rows, page), 0) % t_new
    cq = jnp.sum(jnp.where(t == qt, ck, 0.0), axis=1, keepdims=True)
    kn = _pad_rows(kn_ref[...], page).astype(BF16)
    vn = _pad_rows(vn_ref[...], page).astype(BF16)
    s_new = lax.dot_general(qbd, kn, _NT, preferred_element_type=F32) + cq - ck
    s_new = jnp.where(t <= qt, s_new, NEG_INF)

    lp_all = jnp.concatenate([r[...] for r in lp_refs], axis=0)
    after = (lax.broadcasted_iota(I32, (page, page), 0) > lax.broadcasted_iota(I32, (page, page), 1)).astype(F32)
    within = jnp.dot(lp_all, after, precision=HI, preferred_element_type=F32)
    totals = jnp.sum(lp_all, axis=1, keepdims=True)
    carry = jnp.zeros((N_HEADS, 1), F32)
    biases = [None] * n_pages
    for j in reversed(range(n_pages)):
        later = within[j * N_HEADS:(j + 1) * N_HEADS] + carry
        carry = carry + totals[j * N_HEADS:(j + 1) * N_HEADS]
        biases[j] = jnp.broadcast_to(later[:, None, :], (N_HEADS, t_new, page)).reshape(rows, page)

    kt = jnp.concatenate([r[...].reshape(FOX_DIM, page).astype(BF16) for r in kc_refs], axis=1)
    s_old = jnp.dot(qbd, kt, preferred_element_type=F32) + cq + jnp.concatenate(biases, axis=1)
    p, l = _softmax_terms(jnp.concatenate([s_old, s_new], axis=1))
    n_old = n_pages * page
    vt = jnp.concatenate([r[...].reshape(FOX_DIM, page).astype(BF16) for r in vc_refs], axis=1)
    acc = (lax.dot_general(p[:, :n_old], vt, _NT, preferred_element_type=F32)
           + jnp.dot(p[:, n_old:], vn, preferred_element_type=F32))
    o = acc / l
    col_head = lax.broadcasted_iota(I32, o_ref.shape, 1) // HEAD_DIM
    out = jnp.zeros(o_ref.shape, F32)
    for h in range(N_HEADS):
        out = jnp.where(col_head == h, o[h * t_new:(h + 1) * t_new, :], out)
    o_ref[...] = out


def _fox_decode(layer, pt_flat, n_pages, q3, kn3, vn3, lnt, cache_kt, cache_vt, cache_lt):
    nb, t_new, width = q3.shape
    page = cache_kt.shape[-1]
    assert n_pages <= MAX_DECODE_PAGES
    seq_spec = lambda r, w: pl.BlockSpec((None, r, w), lambda b, pt: (b, 0, 0))

    def cache_spec(j):
        return pl.BlockSpec((None, None, N_HEADS, HEAD_DIM, page), lambda b, pt: (layer, pt[b * n_pages + j], 0, 0, 0))

    def logf_spec(j):
        return pl.BlockSpec((None, None, N_HEADS, page), lambda b, pt: (layer, pt[b * n_pages + j], 0, 0))

    gs = pltpu.PrefetchScalarGridSpec(
        num_scalar_prefetch=1, grid=(nb,),
        in_specs=[seq_spec(t_new, width)] * 3 + [seq_spec(N_HEADS, LANES)]
                 + [cache_spec(j) for j in range(n_pages)] * 2 + [logf_spec(j) for j in range(n_pages)],
        out_specs=seq_spec(t_new, width))
    return pl.pallas_call(
        functools.partial(_fox_dec_kernel, n_pages), grid_spec=gs,
        out_shape=jax.ShapeDtypeStruct((nb, t_new, width), F32),
        compiler_params=_params(1), name="fox_decode",
    )(pt_flat, q3, kn3, vn3, lnt, *([cache_kt] * n_pages), *([cache_vt] * n_pages), *([cache_lt] * n_pages))


def _diff_dec_kernel(n_pages, lam_init, pt_ref, q_ref, kn_ref, vn_ref, *refs):
    kc_refs, vc_refs = refs[:n_pages], refs[n_pages:2 * n_pages]
    lq1_ref, lk1_ref, lq2_ref, lk2_ref, g_ref, o_ref = refs[2 * n_pages:]
    t_new = q_ref.shape[0]
    page = vc_refs[0].shape[0]
    dv = 2 * HEAD_DIM
    head_rows = 2 * t_new
    qbd = _block_diag_queries(q_ref[...], 2 * N_HEADS)
    kn = _pad_rows(kn_ref[...], page).astype(BF16)
    vn = _pad_rows(vn_ref[...], page).astype(BF16)
    s_new = lax.dot_general(qbd, kn, _NT, preferred_element_type=F32)
    t = lax.broadcasted_iota(I32, s_new.shape, 1)
    qt = lax.broadcasted_iota(I32, s_new.shape, 0) % t_new
    s_new = jnp.where(t <= qt, s_new, NEG_INF)
    kt = jnp.concatenate([r[...].reshape(D_MODEL, page).astype(BF16) for r in kc_refs], axis=1)
    s_old = jnp.dot(qbd, kt, preferred_element_type=F32)
    p, l = _softmax_terms(jnp.concatenate([s_old, s_new], axis=1))
    heads = []
    for h in range(N_HEADS):
        v_h = jnp.concatenate([r[:, h, :].astype(BF16) for r in vc_refs] + [vn[:, h * dv:(h + 1) * dv]], axis=0)
        heads.append(jnp.dot(p[h * head_rows:(h + 1) * head_rows], v_h, preferred_element_type=F32))
    acc = jnp.concatenate(heads, axis=0)
    lam = (jnp.exp(jnp.sum(lq1_ref[...] * lk1_ref[...], axis=1, keepdims=True))
           - jnp.exp(jnp.sum(lq2_ref[...] * lk2_ref[...], axis=1, keepdims=True)) + lam_init)
    a = (acc / l).reshape(N_HEADS, 2, t_new, dv)
    o = a[:, 0] - lam * a[:, 1]
    ms = jnp.mean(o * o, axis=-1, keepdims=True)
    o = o * lax.rsqrt(ms + LN_EPS) * g_ref[...] * (1.0 - lam_init)
    for h in range(N_HEADS):
        o_ref[:, h * dv:(h + 1) * dv] = o[h]


def _diff_decode(layer, pt_flat, n_pages, q3, kn3, vn3, cache_kt, cache_v, lam_vecs, subln, lam_init):
    nb, t_new, width = q3.shape
    page = cache_kt.shape[-1]
    dv = 2 * HEAD_DIM
    assert n_pages <= MAX_DECODE_PAGES
    seq_spec = pl.BlockSpec((None, t_new, width), lambda b, pt: (b, 0, 0))

    def k_spec(j):
        return pl.BlockSpec((None, None, 2 * N_HEADS, HEAD_DIM, page), lambda b, pt: (layer, pt[b * n_pages + j], 0, 0, 0))

    def v_spec(j):
        return pl.BlockSpec((None, None, page, N_HEADS, dv), lambda b, pt: (layer, pt[b * n_pages + j], 0, 0, 0))

    gs = pltpu.PrefetchScalarGridSpec(
        num_scalar_prefetch=1, grid=(nb,),
        in_specs=[seq_spec] * 3 + [k_spec(j) for j in range(n_pages)] + [v_spec(j) for j in range(n_pages)]
                 + [_const_spec(a.shape) for a in lam_vecs] + [_const_spec(subln.shape)],
        out_specs=seq_spec)
    return pl.pallas_call(
        functools.partial(_diff_dec_kernel, n_pages, lam_init), grid_spec=gs,
        out_shape=jax.ShapeDtypeStruct((nb, t_new, width), F32),
        compiler_params=_params(1), name="diff_decode",
    )(pt_flat, q3, kn3, vn3, *([cache_kt] * n_pages), *([cache_v] * n_pages), *lam_vecs, subln)


def _conv_kernel(cu_ref, gb_ref, buf_ref, w_ref, o_ref, nb_ref, carry):
    @pl.when(pl.program_id(1) == 0)
    def _():
        carry[...] = buf_ref[...]

    cu = cu_ref[...]
    rows = cu.shape[1]
    t = lax.broadcasted_iota(I32, cu.shape, 1)
    prev1 = carry[:, 1:2, :]
    prev2 = carry[:, 0:1, :]
    back1 = jnp.where(t == 0, prev1, pltpu.roll(cu, 1, axis=1))
    back2 = jnp.where(t == 0, prev2, jnp.where(t == 1, prev1, pltpu.roll(cu, 2, axis=1)))
    y = w_ref[0:1, :] * back2 + w_ref[1:2, :] * back1 + w_ref[2:3, :] * cu
    o_ref[...] = gb_ref[...] * y
    last = cu[:, rows - (CONV_WIDTH - 1):, :]
    carry[...] = last
    nb_ref[...] = last


def _conv(cu3, gb3, buf, w):
    nseq, seq, ch = cu3.shape
    rows = min(512, seq)
    sb = SUBLANES * 2 if seq <= SUBLANES else 1
    sb = min(sb, nseq)
    blk = pl.BlockSpec((sb, rows, ch), lambda b, t: (b, t, 0))
    bufspec = pl.BlockSpec((sb, CONV_WIDTH - 1, ch), lambda b, t: (b, 0, 0))
    return pl.pallas_call(
        _conv_kernel,
        grid=(nseq // sb, seq // rows),
        in_specs=[blk, blk, bufspec, _const_spec(w.shape)],
        out_specs=[blk, bufspec],
        out_shape=[jax.ShapeDtypeStruct(cu3.shape, F32), jax.ShapeDtypeStruct(buf.shape, F32)],
        scratch_shapes=[pltpu.VMEM((sb, CONV_WIDTH - 1, ch), F32)],
        compiler_params=_params(2), name="conv",
    )(cu3, gb3, buf, w)


def _out_ln_kernel(n_parts, *refs):
    parts = refs[:n_parts]
    w_ref, x_ref, g_ref, lg_ref, lb_ref, o_ref = refs[n_parts:]
    y = None
    off = 0
    for p in parts:
        width = p.shape[1]
        d = jnp.dot(p[...].astype(BF16), w_ref[off:off + width, :], preferred_element_type=F32)
        y = d if y is None else y + d
        off += width
    z = ALPHA * x_ref[...] + (1.0 + g_ref[...]) * y
    o_ref[...] = _layer_norm(z, lg_ref[...], lb_ref[...])


def _out_ln(parts, w, x, mod, rows_per_seq, lg, lb):
    n = x.shape[0]
    tm = _row_tile(n)
    row = lambda width: pl.BlockSpec((tm, width), lambda i: (i, 0))
    return pl.pallas_call(
        functools.partial(_out_ln_kernel, len(parts)),
        grid=(n // tm,),
        in_specs=[row(p.shape[1]) for p in parts]
                 + [_const_spec(w.shape), row(D_MODEL), _mod_spec(mod, 2, tm, rows_per_seq),
                    _const_spec(lg.shape), _const_spec(lb.shape)],
        out_specs=row(D_MODEL),
        out_shape=jax.ShapeDtypeStruct((n, D_MODEL), F32),
        compiler_params=_params(1), name="out_ln",
    )(*parts, w, x, mod, lg, lb)


def _router_kernel(x_ref, sc_ref, sh_ref, wr_ref, br_ref, hx_ref, gidx_ref):
    h = x_ref[...] * (1.0 + sc_ref[...]) + sh_ref[...]
    logits = jnp.dot(h, wr_ref[...], precision=HI, preferred_element_type=F32) + br_ref[...]
    lane = lax.broadcasted_iota(I32, logits.shape, 1)
    big = ROUTER_LANES
    gl = jnp.where(lane < N_GROUPS, logits, NEG_INF)
    gmax = jnp.max(gl, axis=1, keepdims=True)
    gidx = jnp.min(jnp.where(gl == gmax, lane, big), axis=1, keepdims=True)
    gsum = jnp.sum(jnp.where(lane < N_GROUPS, jnp.exp(logits - gmax), 0.0), axis=1, keepdims=True)
    g_w = 1.0 / gsum
    in_group = ((lane >= EXPERT_LANE0) & (lane < EXPERT_LANE0 + N_GROUPS * EXPERTS_PER_GROUP)
                & (((lane - EXPERT_LANE0) // EXPERTS_PER_GROUP) == gidx))
    el = jnp.where(in_group, logits, NEG_INF)
    e1 = jnp.max(el, axis=1, keepdims=True)
    i1 = jnp.min(jnp.where(el == e1, lane, big), axis=1, keepdims=True)
    el2 = jnp.where(lane == i1, NEG_INF, el)
    e2 = jnp.max(el2, axis=1, keepdims=True)
    i2 = jnp.min(jnp.where(el2 == e2, lane, big), axis=1, keepdims=True)
    t = jnp.exp(e2 - e1)
    w1 = 1.0 / (1.0 + t)
    w2 = t / (1.0 + t)
    gates = g_w * (jnp.where(lane == i1, w1, 0.0) + jnp.where(lane == i2, w2, 0.0))
    hx_ref[:, :D_MODEL] = h
    hx_ref[:, D_MODEL:] = gates
    gidx_ref[...] = jnp.broadcast_to(gidx, gidx_ref.shape)


def _router(x, mod, rows_per_seq, wr, br):
    n = x.shape[0]
    tm = _row_tile(n)
    row = lambda width: pl.BlockSpec((tm, width), lambda i: (i, 0))
    return pl.pallas_call(
        _router_kernel,
        grid=(n // tm,),
        in_specs=[row(D_MODEL), _mod_spec(mod, 4, tm, rows_per_seq), _mod_spec(mod, 3, tm, rows_per_seq),
                  _const_spec(wr.shape), _const_spec(br.shape)],
        out_specs=[row(D_MODEL + ROUTER_LANES), row(ROUTER_LANES)],
        out_shape=[jax.ShapeDtypeStruct((n, D_MODEL + ROUTER_LANES), F32), jax.ShapeDtypeStruct((n, ROUTER_LANES), I32)],
        compiler_params=_params(1), name="router",
    )(x, mod, mod, wr, br)


def _moe_kernel(src_ref, dst_ref, tg_ref, hx_hbm, w1_ref, w3_ref, w2_ref, y_hbm, hbuf, ybuf, hb_sc, sem_in, sem_out):
    t = pl.program_id(0)
    nt = pl.num_programs(0)
    tm = hb_sc.shape[0]
    n_tok = hx_hbm.shape[0]
    slot = t % 2
    other = 1 - slot

    def gather_row(tile, r, to_slot):
        return pltpu.make_async_copy(hx_hbm.at[pl.ds(src_ref[tile * tm + r], 1), :],
                                     hbuf.at[to_slot, pl.ds(r, 1), :], sem_in.at[to_slot])

    def gather_tile(to_slot):
        return pltpu.make_async_copy(hx_hbm.at[pl.ds(0, tm), :], hbuf.at[to_slot], sem_in.at[to_slot])

    def scatter_row(table_tile, r, from_slot):
        return pltpu.make_async_copy(ybuf.at[from_slot, pl.ds(r, 1), :],
                                     y_hbm.at[pl.ds(dst_ref[table_tile * tm + r], 1), :], sem_out.at[from_slot])

    def scatter_tile(from_slot, row0=0):
        return pltpu.make_async_copy(ybuf.at[from_slot], y_hbm.at[pl.ds(row0, tm), :], sem_out.at[from_slot])

    @pl.when(t == 0)
    def _():
        ybuf[...] = jnp.zeros_like(ybuf)
        for half in range(2):
            scatter_tile(half, n_tok + half * tm).start()
        for half in range(2):
            scatter_tile(half, n_tok + half * tm).wait()

        def first(r, c):
            gather_row(0, r, 0).start()
            return c

        lax.fori_loop(0, tm, first, 0)

    gather_tile(slot).wait()

    @pl.when(t >= 1)
    def _():
        scatter_tile(slot).wait()

    nxt = jnp.minimum(t + 1, nt - 1)
    hb_sc[...] = hbuf[slot, :, :D_MODEL].astype(BF16)
    gates = hbuf[slot, :, D_MODEL:]
    for r in range(tm):
        gather_row(nxt, r, other).start()
    for r in range(tm):
        scatter_row(t, r, other).start()
    group = tg_ref[t]
    h = hb_sc[...]
    lane = lax.broadcasted_iota(I32, gates.shape, 1)
    acc = jnp.zeros((tm, D_MODEL), F32)
    for e in range(EXPERTS_PER_GROUP):
        a = jnp.dot(h, w1_ref[e], preferred_element_type=F32)
        b = jnp.dot(h, w3_ref[e], preferred_element_type=F32)
        hid = (_silu(a) * b).astype(BF16)
        ge = jnp.sum(jnp.where(lane == EXPERT_LANE0 + group * EXPERTS_PER_GROUP + e, gates, 0.0),
                     axis=1, keepdims=True)
        acc = acc + ge * jnp.dot(hid, w2_ref[e], preferred_element_type=F32)
    ybuf[slot] = acc

    @pl.when(t == nt - 1)
    def _():
        def last(r, c):
            scatter_row(t + 1, r, slot).start()
            return c

        lax.fori_loop(0, tm, last, 0)
        scatter_tile(other).wait()
        scatter_tile(slot).wait()
        gather_tile(other).wait()


def _moe(hx, src, dst, tile_group, w1, w3, w2, layer):
    n = hx.shape[0]
    tm = MOE_TILE
    n_tiles = tile_group.shape[0]
    e = EXPERTS_PER_GROUP
    gs = pltpu.PrefetchScalarGridSpec(
        num_scalar_prefetch=3, grid=(n_tiles,),
        in_specs=[pl.BlockSpec(memory_space=pl.ANY),
                  pl.BlockSpec((None, None, e, D_MODEL, D_EXPERT), lambda t, s, d, g: (layer, g[t], 0, 0, 0)),
                  pl.BlockSpec((None, None, e, D_MODEL, D_EXPERT), lambda t, s, d, g: (layer, g[t], 0, 0, 0)),
                  pl.BlockSpec((None, None, e, D_EXPERT, D_MODEL), lambda t, s, d, g: (layer, g[t], 0, 0, 0))],
        out_specs=pl.BlockSpec(memory_space=pl.ANY),
        scratch_shapes=[pltpu.VMEM((2, tm, hx.shape[1]), F32), pltpu.VMEM((2, tm, D_MODEL), F32),
                        pltpu.VMEM((tm, D_MODEL), BF16), pltpu.SemaphoreType.DMA((2,)), pltpu.SemaphoreType.DMA((2,))])
    return pl.pallas_call(
        _moe_kernel, grid_spec=gs,
        out_shape=jax.ShapeDtypeStruct((n + 2 * tm, D_MODEL), F32),
        compiler_params=_params(1), name="moe_experts",
    )(src, dst, tile_group, hx, w1, w3, w2)


def _route_tables(gidx, n):
    tm = MOE_TILE
    n_tiles = (n + N_GROUPS * (tm - 1)) // tm
    n_slots = n_tiles * tm
    onehot = (gidx[:, None] == jnp.arange(N_GROUPS, dtype=I32)[None, :]).astype(I32)
    counts = jnp.sum(onehot, axis=0)
    padded = ((counts + tm - 1) // tm) * tm
    ends = jnp.cumsum(padded)
    starts = ends - padded
    rank = jnp.cumsum(onehot, axis=0) - onehot
    pos = starts[gidx] + jnp.sum(rank * onehot, axis=1)
    token = jnp.full((n_slots,), -1, I32).at[pos].set(jnp.arange(n, dtype=I32))
    slot = jnp.arange(n_slots, dtype=I32)
    spare = n + ((slot // tm) % 2) * tm + slot % tm
    src = jnp.maximum(token, 0)
    dst = jnp.concatenate([n + tm + jnp.arange(tm, dtype=I32), jnp.where(token >= 0, token, spare)])
    tile_start = jnp.arange(n_tiles, dtype=I32) * tm
    tile_group = jnp.minimum(jnp.sum((tile_start[:, None] >= ends[None, :]).astype(I32), axis=1), N_GROUPS - 1)
    return src, dst, tile_group.astype(I32)


def _final_ln_kernel(x_ref, y_ref, g_ref, lg_ref, lb_ref, o_ref):
    z = ALPHA * x_ref[...] + (1.0 + g_ref[...]) * y_ref[...]
    o_ref[...] = _layer_norm(z, lg_ref[...], lb_ref[...])


def _final_ln(x, y, mod, rows_per_seq, lg, lb):
    n = x.shape[0]
    tm = _row_tile(n)
    row = pl.BlockSpec((tm, D_MODEL), lambda i: (i, 0))
    return pl.pallas_call(
        _final_ln_kernel,
        grid=(n // tm,),
        in_specs=[row, row, _mod_spec(mod, 5, tm, rows_per_seq), _const_spec(lg.shape), _const_spec(lb.shape)],
        out_specs=row,
        out_shape=jax.ShapeDtypeStruct((n, D_MODEL), F32),
        compiler_params=_params(1), name="final_ln",
    )(x, y, mod, lg, lb)


def _trunk(x3, mods, past, wts):
    batch, seq, _ = x3.shape
    n = batch * seq
    x = x3.reshape(n, D_MODEL)
    fk, fv, fl, cb, dk, dv = [], [], [], [], [], []
    if past is not None:
        c_fk, c_fv, c_fl, s_conv, c_dk, c_dv, page_table = past
        n_pages = page_table.shape[1]
        pt_flat = page_table.reshape(-1).astype(I32)
        fk_t = jnp.transpose(c_fk, (0, 1, 3, 4, 2))
        fv_t = jnp.transpose(c_fv, (0, 1, 3, 4, 2))
        fl_t = jnp.transpose(c_fl, (0, 1, 3, 2))
        dk_t = jnp.transpose(c_dk, (0, 1, 3, 4, 2))
        per_seq = lambda a: a.reshape(batch, seq, a.shape[-1])
    for l in range(DEPTH):
        i = l // 2
        mod = mods[l]
        if l % 2 == 0:
            qb, kb, vb, k, v, gb, cu, lf = _even_in(x, mod, seq, wts["wm"][i], wts["wf"][i], wts["bf"][i])
            if past is None:
                logf, ctok, ct = _cumsum(lf, batch, seq)
                o_fox = _flash(qb, kb, vb, batch, seq, fox=True, extra=(ctok, ct))
                buf = jnp.zeros((batch, CONV_WIDTH - 1, CONV_DIM), F32)
            else:
                logf = lf[:, :N_HEADS]
                lnt = jnp.pad(jnp.transpose(per_seq(logf), (0, 2, 1)), ((0, 0), (0, 0), (0, LANES - seq)))
                o_fox = _fox_decode(i, pt_flat, n_pages, per_seq(qb), per_seq(k), per_seq(v), lnt,
                                    fk_t, fv_t, fl_t).reshape(n, FOX_DIM)
                buf = s_conv[i]
            o_conv, new_buf = _conv(cu.reshape(batch, seq, CONV_DIM), gb.reshape(batch, seq, CONV_DIM), buf,
                                    wts["conv_w"][i])
            x = _out_ln([o_fox, o_conv.reshape(n, CONV_DIM)], wts["w_out_even"][i], x, mod, seq,
                        wts["ln_g"][l][0], wts["ln_b"][l][0])
            fk.append(k.reshape(batch, seq, N_HEADS, HEAD_DIM))
            fv.append(v.reshape(batch, seq, N_HEADS, HEAD_DIM))
            fl.append(logf.reshape(batch, seq, N_HEADS))
            cb.append(new_buf)
        else:
            lam_init = 0.8 - 0.6 * math.exp(-0.3 * l)
            qb, kb, vb, k, v = _odd_in(x, mod, seq, wts["w_qkv"][i])
            lam_vecs = [wts[name][i] for name in ("lq1", "lk1", "lq2", "lk2")]
            if past is None:
                o = _flash(qb, kb, vb, batch, seq, fox=False, extra=lam_vecs + [wts["subln"][i]], lam_init=lam_init)
            else:
                o = _diff_decode(i, pt_flat, n_pages, per_seq(qb), per_seq(k), per_seq(v), dk_t, c_dv,
                                 lam_vecs, wts["subln"][i], lam_init).reshape(n, D_MODEL)
            x = _out_ln([o], wts["w_out_odd"][i], x, mod, seq, wts["ln_g"][l][0], wts["ln_b"][l][0])
            dk.append(k.reshape(batch, seq, 2 * N_HEADS, HEAD_DIM))
            dv.append(v.reshape(batch, seq, N_HEADS, 2 * HEAD_DIM))
        hx, gidx = _router(x, mod, seq, wts["wr"][l], wts["br"][l])
        src, dst, tile_group = _route_tables(gidx[:, 0], n)
        y = _moe(hx, src, dst, tile_group, wts["w1"], wts["w3"], wts["w2"], l)
        x = _final_ln(x, y, mod, seq, wts["ln_g"][l][1], wts["ln_b"][l][1])
    return (x.reshape(batch, seq, D_MODEL), jnp.stack(fk), jnp.stack(fv), jnp.stack(fl), jnp.stack(cb),
            jnp.stack(dk), jnp.stack(dv))


def _prepare_weights(ln_g, ln_b, w_in_even, b_forget, conv_w, w_out_even, w_qkv_odd, lambda_q1, lambda_k1,
                     lambda_q2, lambda_k2, subln_g, w_out_odd, w_gr, b_gr, w_er, b_er, w1, w3, w2):
    f0, f1 = 3 * FOX_DIM, 3 * FOX_DIM + N_HEADS
    n_exp = N_GROUPS * EXPERTS_PER_GROUP
    pad_lanes = lambda a: jnp.pad(a, [(0, 0)] * (a.ndim - 1) + [(0, LANES - a.shape[-1])])
    wr = jnp.concatenate([w_gr, w_er.transpose(0, 2, 1, 3).reshape(DEPTH, D_MODEL, n_exp)], axis=-1)
    br = jnp.concatenate([b_gr, b_er.reshape(DEPTH, n_exp)], axis=-1)
    grouped = lambda w: w.astype(BF16).reshape(DEPTH, N_GROUPS, EXPERTS_PER_GROUP, w.shape[2], w.shape[3])
    row = lambda a: a[:, None, :]
    return dict(
        ln_g=ln_g[:, :, None, :], ln_b=ln_b[:, :, None, :],
        wm=jnp.concatenate([w_in_even[:, :, :f0], w_in_even[:, :, f1:]], axis=-1).astype(BF16),
        wf=pad_lanes(w_in_even[:, :, f0:f1]).astype(BF16), bf=pad_lanes(b_forget)[:, None, :],
        conv_w=conv_w, w_out_even=w_out_even.astype(BF16), w_qkv=w_qkv_odd.astype(BF16),
        lq1=row(lambda_q1), lk1=row(lambda_k1), lq2=row(lambda_q2), lk2=row(lambda_k2), subln=row(subln_g),
        w_out_odd=w_out_odd.astype(BF16), wr=pad_lanes(wr), br=pad_lanes(br)[:, None, :],
        w1=grouped(w1), w3=grouped(w3), w2=grouped(w2))


def kernel(x_prompt, x_sample, cache_fox_k, cache_fox_v, cache_fox_logf, state_conv, cache_diff_k, cache_diff_v,
           page_table, c_prompt, c_sample, ada_w, ada_b, ln_g, ln_b, w_in_even, b_forget, conv_w, w_out_even,
           w_qkv_odd, lambda_q1, lambda_k1, lambda_q2, lambda_k2, subln_g, w_out_odd, w_gr, b_gr, w_er, b_er,
           w1, w3, w2):
    wts = _prepare_weights(ln_g, ln_b, w_in_even, b_forget, conv_w, w_out_even, w_qkv_odd, lambda_q1, lambda_k1,
                           lambda_q2, lambda_k2, subln_g, w_out_odd, w_gr, b_gr, w_er, b_er, w1, w3, w2)
    nbp, nbs = c_prompt.shape[0], c_sample.shape[0]
    n_rows = nbp + nbs
    pad = (-n_rows) % SUBLANES
    c_all = jnp.concatenate([c_prompt, c_sample, jnp.zeros((pad, D_MODEL), F32)], axis=0)
    mod_all = _adaln(c_all, ada_w, ada_b)
    mods_p = [mod_all[l, :nbp][:, None, :] for l in range(DEPTH)]
    mods_s = [jnp.repeat(mod_all[l, nbp:n_rows], x_sample.shape[1], axis=0) for l in range(DEPTH)]
    y_p, fk_p, fv_p, fl_p, cb_p, dk_p, dv_p = _trunk(x_prompt, mods_p, None, wts)
    past = (cache_fox_k, cache_fox_v, cache_fox_logf, state_conv, cache_diff_k, cache_diff_v, page_table)
    y_s, fk_s, fv_s, fl_s, cb_s, dk_s, dv_s = _trunk(x_sample, mods_s, past, wts)
    return (y_p, y_s, fk_p, fv_p, fl_p, cb_p, dk_p, dv_p, fk_s, fv_s, fl_s, cb_s, dk_s, dv_s)
```

```python
import functools
import math

import jax
import jax.numpy as jnp
from jax import lax
from jax.experimental import pallas as pl
from jax.experimental.pallas import tpu as pltpu

F32, BF16, I32 = jnp.float32, jnp.bfloat16, jnp.int32
HI = lax.Precision.HIGHEST

D_MODEL = 1024
DEPTH = 4
HEAD_DIM = 64
N_HEADS = 8
FOX_DIM = N_HEADS * HEAD_DIM
CONV_DIM = D_MODEL - FOX_DIM
CONV_WIDTH = 3
N_GROUPS = 4
EXPERTS_PER_GROUP = 8
D_EXPERT = D_MODEL // 4
ALPHA = (2 * DEPTH) ** 0.25
LN_EPS = 1e-5
NEG_INF = -1e30
QK_SCALE = HEAD_DIM ** -0.5

LANES = 128
SUBLANES = 8
VMEM_LIMIT = 56 * 1024 * 1024
ROUTER_LANES = LANES
EXPERT_LANE0 = N_GROUPS
MOE_TILE = 256

_NT = (((1,), (1,)), ((), ()))


def _params(n_grid):
    return pltpu.CompilerParams(dimension_semantics=("arbitrary",) * n_grid, vmem_limit_bytes=VMEM_LIMIT)


def _row_tile(n):
    return min(512, n)


def _mod_spec(mod, chunk, tm, rows_per_seq):
    if mod.ndim == 3:
        tiles_per_seq = rows_per_seq // tm
        return pl.BlockSpec((None, 1, D_MODEL), lambda i: (i // tiles_per_seq, 0, chunk))
    return pl.BlockSpec((tm, D_MODEL), lambda i: (i, chunk))


def _const_spec(shape):
    return pl.BlockSpec(shape, lambda *_: (0,) * len(shape))


def _layer_norm(z, g, b):
    mu = jnp.mean(z, axis=-1, keepdims=True)
    zc = z - mu
    var = jnp.mean(zc * zc, axis=-1, keepdims=True)
    return zc * lax.rsqrt(var + LN_EPS) * g + b


def _log_sigmoid(x):
    return jnp.minimum(x, 0.0) - jnp.log1p(jnp.exp(-jnp.abs(x)))


def _silu(x):
    return x * jax.nn.sigmoid(x)


def _adaln_kernel(c_ref, w_ref, b_ref, o_ref):
    a = _silu(c_ref[...]).astype(BF16)
    o_ref[...] = jnp.dot(a, w_ref[...].astype(BF16), preferred_element_type=F32) + b_ref[...]


def _adaln(c_all, ada_w, ada_b):
    rows = c_all.shape[0]
    tn = 1536
    return pl.pallas_call(
        _adaln_kernel,
        grid=(DEPTH, 6 * D_MODEL // tn),
        in_specs=[pl.BlockSpec((rows, D_MODEL), lambda l, j: (0, 0)),
                  pl.BlockSpec((None, D_MODEL, tn), lambda l, j: (l, 0, j)),
                  pl.BlockSpec((None, 1, tn), lambda l, j: (l, 0, j))],
        out_specs=pl.BlockSpec((None, rows, tn), lambda l, j: (l, 0, j)),
        out_shape=jax.ShapeDtypeStruct((DEPTH, rows, 6 * D_MODEL), F32),
        compiler_params=_params(2), name="adaln",
    )(c_all, ada_w, ada_b.reshape(DEPTH, 1, 6 * D_MODEL))


def _even_in_kernel(x_ref, sc_ref, sh_ref, wm_ref, wf_ref, bf_ref,
                    qb_ref, kb_ref, vb_ref, k_ref, v_ref, gb_ref, cu_ref, lf_ref):
    h = (x_ref[...] * (1.0 + sc_ref[...]) + sh_ref[...]).astype(BF16)

    def mm(c):
        return jnp.dot(h, wm_ref[:, c * FOX_DIM:(c + 1) * FOX_DIM], preferred_element_type=F32)

    qb_ref[...] = (mm(0) * QK_SCALE).astype(BF16)
    k = mm(1)
    k_ref[...] = k
    kb_ref[...] = k.astype(BF16)
    v = mm(2)
    v_ref[...] = v
    vb_ref[...] = v.astype(BF16)
    gb_ref[...] = mm(3)
    cu_ref[...] = mm(4) * mm(5)
    f = jnp.dot(h, wf_ref[...], preferred_element_type=F32) + bf_ref[...]
    lane = lax.broadcasted_iota(I32, f.shape, 1)
    lf_ref[...] = jnp.where(lane < N_HEADS, _log_sigmoid(f), 0.0)


def _even_in(x, mod, rows_per_seq, wm, wf, bfp):
    n = x.shape[0]
    tm = _row_tile(n)
    row = lambda w: pl.BlockSpec((tm, w), lambda i: (i, 0))
    sds = lambda w, dt: jax.ShapeDtypeStruct((n, w), dt)
    return pl.pallas_call(
        _even_in_kernel,
        grid=(n // tm,),
        in_specs=[row(D_MODEL), _mod_spec(mod, 1, tm, rows_per_seq), _mod_spec(mod, 0, tm, rows_per_seq),
                  _const_spec(wm.shape), _const_spec(wf.shape), _const_spec(bfp.shape)],
        out_specs=[row(FOX_DIM)] * 7 + [row(LANES)],
        out_shape=[sds(FOX_DIM, BF16)] * 3 + [sds(FOX_DIM, F32)] * 4 + [sds(LANES, F32)],
        compiler_params=_params(1), name="even_in",
    )(x, mod, mod, wm, wf, bfp)


def _odd_in_kernel(x_ref, sc_ref, sh_ref, w_ref, qb_ref, kb_ref, vb_ref, k_ref, v_ref):
    h = (x_ref[...] * (1.0 + sc_ref[...]) + sh_ref[...]).astype(BF16)

    def mm(c):
        return jnp.dot(h, w_ref[:, c * D_MODEL:(c + 1) * D_MODEL], preferred_element_type=F32)

    qb_ref[...] = (mm(0) * QK_SCALE).astype(BF16)
    k = mm(1)
    k_ref[...] = k
    kb_ref[...] = k.astype(BF16)
    v = mm(2)
    v_ref[...] = v
    vb_ref[...] = v.astype(BF16)


def _odd_in(x, mod, rows_per_seq, w):
    n = x.shape[0]
    tm = _row_tile(n)
    row = pl.BlockSpec((tm, D_MODEL), lambda i: (i, 0))
    sds = lambda dt: jax.ShapeDtypeStruct((n, D_MODEL), dt)
    return pl.pallas_call(
        _odd_in_kernel,
        grid=(n // tm,),
        in_specs=[row, _mod_spec(mod, 1, tm, rows_per_seq), _mod_spec(mod, 0, tm, rows_per_seq), _const_spec(w.shape)],
        out_specs=[row] * 5,
        out_shape=[sds(BF16)] * 3 + [sds(F32)] * 2,
        compiler_params=_params(1), name="odd_in",
    )(x, mod, mod, w)


def _cumsum_kernel(lf_ref, logf_ref, ctok_ref, ct_ref, carry):
    @pl.when(pl.program_id(1) == 0)
    def _():
        carry[...] = jnp.zeros_like(carry)

    lf = lf_ref[...]
    tc = lf.shape[0]
    r = lax.broadcasted_iota(I32, (tc, tc), 0)
    c = lax.broadcasted_iota(I32, (tc, tc), 1)
    tri = (c <= r).astype(F32)
    cs = jnp.dot(tri, lf, precision=HI, preferred_element_type=F32) + carry[...]
    carry[...] = cs[tc - 1:tc, :]
    logf_ref[...] = lf[:, :N_HEADS]
    ctok_ref[...] = cs
    eye = (lax.broadcasted_iota(I32, (N_HEADS, LANES), 0) == lax.broadcasted_iota(I32, (N_HEADS, LANES), 1)).astype(F32)
    ct_ref[...] = lax.dot_general(eye, cs, _NT, precision=HI, preferred_element_type=F32)


def _cumsum(lf, batch, seq):
    n = lf.shape[0]
    tc = min(512, seq)
    nt = seq // tc
    return pl.pallas_call(
        _cumsum_kernel,
        grid=(batch, nt),
        in_specs=[pl.BlockSpec((tc, LANES), lambda b, t: (b * nt + t, 0))],
        out_specs=[pl.BlockSpec((tc, N_HEADS), lambda b, t: (b * nt + t, 0)),
                   pl.BlockSpec((tc, LANES), lambda b, t: (b * nt + t, 0)),
                   pl.BlockSpec((None, N_HEADS, tc), lambda b, t: (b, 0, t))],
        out_shape=[jax.ShapeDtypeStruct((n, N_HEADS), F32), jax.ShapeDtypeStruct((n, LANES), F32),
                   jax.ShapeDtypeStruct((batch, N_HEADS, seq), F32)],
        scratch_shapes=[pltpu.VMEM((1, LANES), F32)],
        compiler_params=_params(2), name="logf_cumsum",
    )(lf)


def _online_update(state, s, v):
    m, l, acc = state
    m_new = jnp.maximum(m, jnp.max(s, axis=1, keepdims=True))
    alpha = jnp.exp(m - m_new)
    p = jnp.exp(s - m_new)
    l = alpha * l + jnp.sum(p, axis=1, keepdims=True)
    acc = alpha * acc + jnp.dot(p.astype(BF16), v, preferred_element_type=F32)
    return m_new, l, acc


def _flash_kernel(fox, lam_init, tile, *refs):
    if fox:
        q_ref, k_ref, v_ref, ctok_ref, ck_ref, o_ref = refs
    else:
        q_ref, k_ref, v_ref, lq1_ref, lk1_ref, lq2_ref, lk2_ref, g_ref, o_ref = refs
    i = pl.program_id(2)
    q = q_ref[...]
    lane = lax.broadcasted_iota(I32, q.shape, 1)
    low = lane < HEAD_DIM
    zero = jnp.zeros_like(q)
    q_a = jnp.where(low, q, zero)
    q_b = jnp.where(low, zero, q)
    if fox:
        hp = pl.program_id(1)
        ctok = ctok_ref[...]
        cq_a = jnp.sum(jnp.where(lane == 2 * hp, ctok, 0.0), axis=1, keepdims=True)
        cq_b = jnp.sum(jnp.where(lane == 2 * hp + 1, ctok, 0.0), axis=1, keepdims=True)

    def scores(j, diagonal):
        start = pl.multiple_of(j * tile, tile)
        kj = k_ref[pl.ds(start, tile), :]
        vj = v_ref[pl.ds(start, tile), :]
        s_a = lax.dot_general(q_a, kj, _NT, preferred_element_type=F32)
        s_b = lax.dot_general(q_b, kj, _NT, preferred_element_type=F32)
        if fox:
            s_a = s_a + cq_a - ck_ref[0:1, pl.ds(start, tile)]
            s_b = s_b + cq_b - ck_ref[1:2, pl.ds(start, tile)]
        if diagonal:
            visible = lax.broadcasted_iota(I32, s_a.shape, 1) <= lax.broadcasted_iota(I32, s_a.shape, 0)
            s_a = jnp.where(visible, s_a, NEG_INF)
            s_b = jnp.where(visible, s_b, NEG_INF)
        return s_a, s_b, vj

    def step(j, state, diagonal=False):
        s_a, s_b, vj = scores(j, diagonal)
        return _online_update(state[0], s_a, vj), _online_update(state[1], s_b, vj)

    tq = q.shape[0]
    init = (jnp.full((tq, 1), NEG_INF, F32), jnp.zeros((tq, 1), F32), jnp.zeros((tq, LANES), F32))
    state = lax.fori_loop(0, i, step, (init, init))
    (_, l_a, acc_a), (_, l_b, acc_b) = step(i, state, diagonal=True)
    if fox:
        o_ref[...] = jnp.where(low, acc_a / l_a, acc_b / l_b)
    else:
        lam = (jnp.exp(jnp.sum(lq1_ref[...] * lk1_ref[...], axis=1, keepdims=True))
               - jnp.exp(jnp.sum(lq2_ref[...] * lk2_ref[...], axis=1, keepdims=True)) + lam_init)
        o = acc_a / l_a - lam * (acc_b / l_b)
        ms = jnp.mean(o * o, axis=1, keepdims=True)
        o_ref[...] = o * lax.rsqrt(ms + LN_EPS) * g_ref[...] * (1.0 - lam_init)


def _flash(qb, kb, vb, batch, seq, *, fox, extra, lam_init=0.0):
    n, width = qb.shape
    tile = min(512, seq)
    nq = seq // tile
    blocks = width // LANES
    qspec = pl.BlockSpec((tile, LANES), lambda b, h, i: (b * nq + i, h))
    kvspec = pl.BlockSpec((seq, LANES), lambda b, h, i: (b, h))
    if fox:
        ctok, ct = extra
        especs = [pl.BlockSpec((tile, LANES), lambda b, h, i: (b * nq + i, 0)),
                  pl.BlockSpec((None, None, 2, seq), lambda b, h, i: (b, h, 0, 0))]
        eargs = [ctok, ct.reshape(batch, blocks, 2, seq)]
    else:
        especs = [_const_spec(a.shape) for a in extra]
        eargs = list(extra)
    return pl.pallas_call(
        functools.partial(_flash_kernel, fox, lam_init, tile),
        grid=(batch, blocks, nq),
        in_specs=[qspec, kvspec, kvspec] + especs,
        out_specs=qspec,
        out_shape=jax.ShapeDtypeStruct((n, width), F32),
        compiler_params=_params(3), name="flash_fox" if fox else "flash_diff",
    )(qb, kb, vb, *eargs)


MAX_DECODE_PAGES = 16


def _block_diag_queries(q, n_blocks):
    t, width = q.shape
    rows = n_blocks * t
    qf = jnp.broadcast_to(q.astype(F32)[None], (n_blocks, t, width)).reshape(rows, width)
    r = lax.broadcasted_iota(I32, (rows, width), 0)
    c = lax.broadcasted_iota(I32, (rows, width), 1)
    return jnp.where(c // HEAD_DIM == r // t, qf, 0.0).astype(BF16)


def _pad_rows(a, rows):
    return jnp.concatenate([a, jnp.zeros((rows - a.shape[0], a.shape[1]), a.dtype)], axis=0)


def _softmax_terms(s):
    m = jnp.max(s, axis=1, keepdims=True)
    p = jnp.exp(s - m)
    return p.astype(BF16), jnp.sum(p, axis=1, keepdims=True)


def _fox_dec_kernel(n_pages, pt_ref, q_ref, kn_ref, vn_ref, lnt_ref, *refs):
    kc_refs, vc_refs, lp_refs = refs[:n_pages], refs[n_pages:2 * n_pages], refs[2 * n_pages:3 * n_pages]
    o_ref = refs[3 * n_pages]
    t_new = q_ref.shape[0]
    rows = N_HEADS * t_new
    page = lp_refs[0].shape[1]
    qbd = _block_diag_queries(q_ref[...], N_HEADS)

    x = lnt_ref[...]
    lane = lax.broadcasted_iota(I32, x.shape, 1)
    sh = 1
    while sh < t_new:
        x = x + jnp.where(lane >= sh, pltpu.roll(x, sh, axis=1), 0.0)
        sh *= 2
    ck = jnp.broadcast_to(x[:, None, :], (N_HEADS, t_new, page)).reshape(rows, page)
    t = lax.broadcasted_iota(I32, (rows, page), 1)
    qt = lax.broadcasted_iota(I32, (rows, page), 0) % t_new
    cq = jnp.sum(jnp.where(t == qt, ck, 0.0), axis=1, keepdims=True)
    kn = _pad_rows(kn_ref[...], page).astype(BF16)
    vn = _pad_rows(vn_ref[...], page).astype(BF16)
    s_new = lax.dot_general(qbd, kn, _NT, preferred_element_type=F32) + cq - ck
    s_new = jnp.where(t <= qt, s_new, NEG_INF)

    lp_all = jnp.concatenate([r[...] for r in lp_refs], axis=0)
    after = (lax.broadcasted_iota(I32, (page, page), 0) > lax.broadcasted_iota(I32, (page, page), 1)).astype(F32)
    within = jnp.dot(lp_all, after, precision=HI, preferred_element_type=F32)
    totals = jnp.sum(lp_all, axis=1, keepdims=True)
    carry = jnp.zeros((N_HEADS, 1), F32)
    biases = [None] * n_pages
    for j in reversed(range(n_pages)):
        later = within[j * N_HEADS:(j + 1) * N_HEADS] + carry
        carry = carry + totals[j * N_HEADS:(j + 1) * N_HEADS]
        biases[j] = jnp.broadcast_to(later[:, None, :], (N_HEADS, t_new, page)).reshape(rows, page)

    kt = jnp.concatenate([r[...].reshape(FOX_DIM, page).astype(BF16) for r in kc_refs], axis=1)
    s_old = jnp.dot(qbd, kt, preferred_element_type=F32) + cq + jnp.concatenate(biases, axis=1)
    p, l = _softmax_terms(jnp.concatenate([s_old, s_new], axis=1))
    n_old = n_pages * page
    vt = jnp.concatenate([r[...].reshape(FOX_DIM, page).astype(BF16) for r in vc_refs], axis=1)
    acc = (lax.dot_general(p[:, :n_old], vt, _NT, preferred_element_type=F32)
           + jnp.dot(p[:, n_old:], vn, preferred_element_type=F32))
    o = acc / l
    col_head = lax.broadcasted_iota(I32, o_ref.shape, 1) // HEAD_DIM
    out = jnp.zeros(o_ref.shape, F32)
    for h in range(N_HEADS):
        out = jnp.where(col_head == h, o[h * t_new:(h + 1) * t_new, :], out)
    o_ref[...] = out


def _fox_decode(layer, pt_flat, n_pages, q3, kn3, vn3, lnt, cache_kt, cache_vt, cache_lt):
    nb, t_new, width = q3.shape
    page = cache_kt.shape[-1]
    assert n_pages <= MAX_DECODE_PAGES
    seq_spec = lambda r, w: pl.BlockSpec((None, r, w), lambda b, pt: (b, 0, 0))

    def cache_spec(j):
        return pl.BlockSpec((None, None, N_HEADS, HEAD_DIM, page), lambda b, pt: (layer, pt[b * n_pages + j], 0, 0, 0))

    def logf_spec(j):
        return pl.BlockSpec((None, None, N_HEADS, page), lambda b, pt: (layer, pt[b * n_pages + j], 0, 0))

    gs = pltpu.PrefetchScalarGridSpec(
        num_scalar_prefetch=1, grid=(nb,),
        in_specs=[seq_spec(t_new, width)] * 3 + [seq_spec(N_HEADS, LANES)]
                 + [cache_spec(j) for j in range(n_pages)] * 2 + [logf_spec(j) for j in range(n_pages)],
        out_specs=seq_spec(t_new, width))
    return pl.pallas_call(
        functools.partial(_fox_dec_kernel, n_pages), grid_spec=gs,
        out_shape=jax.ShapeDtypeStruct((nb, t_new, width), F32),
        compiler_params=_params(1), name="fox_decode",
    )(pt_flat, q3, kn3, vn3, lnt, *([cache_kt] * n_pages), *([cache_vt] * n_pages), *([cache_lt] * n_pages))


def _diff_dec_kernel(n_pages, lam_init, pt_ref, q_ref, kn_ref, vn_ref, *refs):
    kc_refs, vc_refs = refs[:n_pages], refs[n_pages:2 * n_pages]
    lq1_ref, lk1_ref, lq2_ref, lk2_ref, g_ref, o_ref = refs[2 * n_pages:]
    t_new = q_ref.shape[0]
    page = kc_refs[0].shape[-1]
    dv = 2 * HEAD_DIM
    head_rows = 2 * t_new
    qbd = _block_diag_queries(q_ref[...], 2 * N_HEADS)
    kn = _pad_rows(kn_ref[...], page).astype(BF16)
    vn = _pad_rows(vn_ref[...], page).astype(BF16)
    s_new = lax.dot_general(qbd, kn, _NT, preferred_element_type=F32)
    t = lax.broadcasted_iota(I32, s_new.shape, 1)
    qt = lax.broadcasted_iota(I32, s_new.shape, 0) % t_new
    s_new = jnp.where(t <= qt, s_new, NEG_INF)
    kt = jnp.concatenate([r[...].reshape(D_MODEL, page).astype(BF16) for r in kc_refs], axis=1)
    s_old = jnp.dot(qbd, kt, preferred_element_type=F32)
    p, l = _softmax_terms(jnp.concatenate([s_old, s_new], axis=1))
    heads = []
    for h in range(N_HEADS):
        v_h = jnp.concatenate([r[pl.ds(h, page, stride=N_HEADS), :].astype(BF16) for r in vc_refs]
                              + [vn[:, h * dv:(h + 1) * dv]], axis=0)
        heads.append(jnp.dot(p[h * head_rows:(h + 1) * head_rows], v_h, preferred_element_type=F32))
    acc = jnp.concatenate(heads, axis=0)
    lam = (jnp.exp(jnp.sum(lq1_ref[...] * lk1_ref[...], axis=1, keepdims=True))
           - jnp.exp(jnp.sum(lq2_ref[...] * lk2_ref[...], axis=1, keepdims=True)) + lam_init)
    a = (acc / l).reshape(N_HEADS, 2, t_new, dv)
    o = a[:, 0] - lam * a[:, 1]
    ms = jnp.mean(o * o, axis=-1, keepdims=True)
    o = o * lax.rsqrt(ms + LN_EPS) * g_ref[...] * (1.0 - lam_init)
    for h in range(N_HEADS):
        o_ref[:, h * dv:(h + 1) * dv] = o[h]


def _diff_decode(layer, pt_flat, n_pages, q3, kn3, vn3, cache_kt, cache_v, lam_vecs, subln, lam_init):
    nb, t_new, width = q3.shape
    page = cache_kt.shape[-1]
    dv = 2 * HEAD_DIM
    assert n_pages <= MAX_DECODE_PAGES
    seq_spec = pl.BlockSpec((None, t_new, width), lambda b, pt: (b, 0, 0))

    def k_spec(j):
        return pl.BlockSpec((None, None, 2 * N_HEADS, HEAD_DIM, page), lambda b, pt: (layer, pt[b * n_pages + j], 0, 0, 0))

    def v_spec(j):
        return pl.BlockSpec((None, None, page * N_HEADS, dv), lambda b, pt: (layer, pt[b * n_pages + j], 0, 0))

    gs = pltpu.PrefetchScalarGridSpec(
        num_scalar_prefetch=1, grid=(nb,),
        in_specs=[seq_spec] * 3 + [k_spec(j) for j in range(n_pages)] + [v_spec(j) for j in range(n_pages)]
                 + [_const_spec(a.shape) for a in lam_vecs] + [_const_spec(subln.shape)],
        out_specs=seq_spec)
    return pl.pallas_call(
        functools.partial(_diff_dec_kernel, n_pages, lam_init), grid_spec=gs,
        out_shape=jax.ShapeDtypeStruct((nb, t_new, width), F32),
        compiler_params=_params(1), name="diff_decode",
    )(pt_flat, q3, kn3, vn3, *([cache_kt] * n_pages), *([cache_v] * n_pages), *lam_vecs, subln)


def _conv_kernel(cu_ref, gb_ref, buf_ref, w_ref, o_ref, nb_ref, carry):
    @pl.when(pl.program_id(1) == 0)
    def _():
        carry[...] = buf_ref[...]

    cu = cu_ref[...]
    rows = cu.shape[1]
    t = lax.broadcasted_iota(I32, cu.shape, 1)
    prev1 = carry[:, 1:2, :]
    prev2 = carry[:, 0:1, :]
    back1 = jnp.where(t == 0, prev1, pltpu.roll(cu, 1, axis=1))
    back2 = jnp.where(t == 0, prev2, jnp.where(t == 1, prev1, pltpu.roll(cu, 2, axis=1)))
    y = w_ref[0:1, :] * back2 + w_ref[1:2, :] * back1 + w_ref[2:3, :] * cu
    o_ref[...] = gb_ref[...] * y
    last = cu[:, rows - (CONV_WIDTH - 1):, :]
    carry[...] = last
    nb_ref[...] = last


def _conv(cu3, gb3, buf, w):
    nseq, seq, ch = cu3.shape
    rows = min(512, seq)
    sb = SUBLANES * 2 if seq <= SUBLANES else 1
    sb = min(sb, nseq)
    blk = pl.BlockSpec((sb, rows, ch), lambda b, t: (b, t, 0))
    bufspec = pl.BlockSpec((sb, CONV_WIDTH - 1, ch), lambda b, t: (b, 0, 0))
    return pl.pallas_call(
        _conv_kernel,
        grid=(nseq // sb, seq // rows),
        in_specs=[blk, blk, bufspec, _const_spec(w.shape)],
        out_specs=[blk, bufspec],
        out_shape=[jax.ShapeDtypeStruct(cu3.shape, F32), jax.ShapeDtypeStruct(buf.shape, F32)],
        scratch_shapes=[pltpu.VMEM((sb, CONV_WIDTH - 1, ch), F32)],
        compiler_params=_params(2), name="conv",
    )(cu3, gb3, buf, w)


def _out_ln_kernel(n_parts, *refs):
    parts = refs[:n_parts]
    w_ref, x_ref, g_ref, lg_ref, lb_ref, o_ref = refs[n_parts:]
    y = None
    off = 0
    for p in parts:
        width = p.shape[1]
        d = jnp.dot(p[...].astype(BF16), w_ref[off:off + width, :], preferred_element_type=F32)
        y = d if y is None else y + d
        off += width
    z = ALPHA * x_ref[...] + (1.0 + g_ref[...]) * y
    o_ref[...] = _layer_norm(z, lg_ref[...], lb_ref[...])


def _out_ln(parts, w, x, mod, rows_per_seq, lg, lb):
    n = x.shape[0]
    tm = _row_tile(n)
    row = lambda width: pl.BlockSpec((tm, width), lambda i: (i, 0))
    return pl.pallas_call(
        functools.partial(_out_ln_kernel, len(parts)),
        grid=(n // tm,),
        in_specs=[row(p.shape[1]) for p in parts]
                 + [_const_spec(w.shape), row(D_MODEL), _mod_spec(mod, 2, tm, rows_per_seq),
                    _const_spec(lg.shape), _const_spec(lb.shape)],
        out_specs=row(D_MODEL),
        out_shape=jax.ShapeDtypeStruct((n, D_MODEL), F32),
        compiler_params=_params(1), name="out_ln",
    )(*parts, w, x, mod, lg, lb)


def _router_kernel(x_ref, sc_ref, sh_ref, wr_ref, br_ref, hx_ref, gidx_ref):
    h = x_ref[...] * (1.0 + sc_ref[...]) + sh_ref[...]
    logits = jnp.dot(h, wr_ref[...], precision=HI, preferred_element_type=F32) + br_ref[...]
    lane = lax.broadcasted_iota(I32, logits.shape, 1)
    big = ROUTER_LANES
    gl = jnp.where(lane < N_GROUPS, logits, NEG_INF)
    gmax = jnp.max(gl, axis=1, keepdims=True)
    gidx = jnp.min(jnp.where(gl == gmax, lane, big), axis=1, keepdims=True)
    gsum = jnp.sum(jnp.where(lane < N_GROUPS, jnp.exp(logits - gmax), 0.0), axis=1, keepdims=True)
    g_w = 1.0 / gsum
    in_group = ((lane >= EXPERT_LANE0) & (lane < EXPERT_LANE0 + N_GROUPS * EXPERTS_PER_GROUP)
                & (((lane - EXPERT_LANE0) // EXPERTS_PER_GROUP) == gidx))
    el = jnp.where(in_group, logits, NEG_INF)
    e1 = jnp.max(el, axis=1, keepdims=True)
    i1 = jnp.min(jnp.where(el == e1, lane, big), axis=1, keepdims=True)
    el2 = jnp.where(lane == i1, NEG_INF, el)
    e2 = jnp.max(el2, axis=1, keepdims=True)
    i2 = jnp.min(jnp.where(el2 == e2, lane, big), axis=1, keepdims=True)
    t = jnp.exp(e2 - e1)
    w1 = 1.0 / (1.0 + t)
    w2 = t / (1.0 + t)
    gates = g_w * (jnp.where(lane == i1, w1, 0.0) + jnp.where(lane == i2, w2, 0.0))
    hx_ref[:, :D_MODEL] = h
    hx_ref[:, D_MODEL:] = gates
    gidx_ref[...] = jnp.broadcast_to(gidx, gidx_ref.shape)


def _router(x, mod, rows_per_seq, wr, br):
    n = x.shape[0]
    tm = _row_tile(n)
    row = lambda width: pl.BlockSpec((tm, width), lambda i: (i, 0))
    return pl.pallas_call(
        _router_kernel,
        grid=(n // tm,),
        in_specs=[row(D_MODEL), _mod_spec(mod, 4, tm, rows_per_seq), _mod_spec(mod, 3, tm, rows_per_seq),
                  _const_spec(wr.shape), _const_spec(br.shape)],
        out_specs=[row(D_MODEL + ROUTER_LANES), row(ROUTER_LANES)],
        out_shape=[jax.ShapeDtypeStruct((n, D_MODEL + ROUTER_LANES), F32), jax.ShapeDtypeStruct((n, ROUTER_LANES), I32)],
        compiler_params=_params(1), name="router",
    )(x, mod, mod, wr, br)


def _moe_kernel(src_ref, dst_ref, tg_ref, hx_hbm, w1_ref, w3_ref, w2_ref, y_hbm, hbuf, ybuf, hb_sc, sem_in, sem_out):
    t = pl.program_id(0)
    nt = pl.num_programs(0)
    tm = hb_sc.shape[0]
    n_tok = hx_hbm.shape[0]
    slot = t % 2
    other = 1 - slot

    def gather_row(tile, r, to_slot):
        return pltpu.make_async_copy(hx_hbm.at[pl.ds(src_ref[tile * tm + r], 1), :],
                                     hbuf.at[to_slot, pl.ds(r, 1), :], sem_in.at[to_slot])

    def gather_tile(to_slot):
        return pltpu.make_async_copy(hx_hbm.at[pl.ds(0, tm), :], hbuf.at[to_slot], sem_in.at[to_slot])

    def scatter_row(table_tile, r, from_slot):
        return pltpu.make_async_copy(ybuf.at[from_slot, pl.ds(r, 1), :],
                                     y_hbm.at[pl.ds(dst_ref[table_tile * tm + r], 1), :], sem_out.at[from_slot])

    def scatter_tile(from_slot, row0=0):
        return pltpu.make_async_copy(ybuf.at[from_slot], y_hbm.at[pl.ds(row0, tm), :], sem_out.at[from_slot])

    @pl.when(t == 0)
    def _():
        ybuf[...] = jnp.zeros_like(ybuf)
        for half in range(2):
            scatter_tile(half, n_tok + half * tm).start()
        for half in range(2):
            scatter_tile(half, n_tok + half * tm).wait()

        def first(r, c):
            gather_row(0, r, 0).start()
            return c

        lax.fori_loop(0, tm, first, 0)

    gather_tile(slot).wait()

    @pl.when(t >= 1)
    def _():
        scatter_tile(slot).wait()

    nxt = jnp.minimum(t + 1, nt - 1)
    hb_sc[...] = hbuf[slot, :, :D_MODEL].astype(BF16)
    gates = hbuf[slot, :, D_MODEL:]
    for r in range(tm):
        gather_row(nxt, r, other).start()
    for r in range(tm):
        scatter_row(t, r, other).start()
    group = tg_ref[t]
    h = hb_sc[...]
    lane = lax.broadcasted_iota(I32, gates.shape, 1)
    acc = jnp.zeros((tm, D_MODEL), F32)
    for e in range(EXPERTS_PER_GROUP):
        a = jnp.dot(h, w1_ref[e], preferred_element_type=F32)
        b = jnp.dot(h, w3_ref[e], preferred_element_type=F32)
        hid = (_silu(a) * b).astype(BF16)
        ge = jnp.sum(jnp.where(lane == EXPERT_LANE0 + group * EXPERTS_PER_GROUP + e, gates, 0.0),
                     axis=1, keepdims=True)
        acc = acc + ge * jnp.dot(hid, w2_ref[e], preferred_element_type=F32)
    ybuf[slot] = acc

    @pl.when(t == nt - 1)
    def _():
        def last(r, c):
            scatter_row(t + 1, r, slot).start()
            return c

        lax.fori_loop(0, tm, last, 0)
        scatter_tile(other).wait()
        scatter_tile(slot).wait()
        gather_tile(other).wait()


def _moe(hx, src, dst, tile_group, w1, w3, w2, layer):
    n = hx.shape[0]
    tm = MOE_TILE
    n_tiles = tile_group.shape[0]
    e = EXPERTS_PER_GROUP
    gs = pltpu.PrefetchScalarGridSpec(
        num_scalar_prefetch=3, grid=(n_tiles,),
        in_specs=[pl.BlockSpec(memory_space=pl.ANY),
                  pl.BlockSpec((None, None, e, D_MODEL, D_EXPERT), lambda t, s, d, g: (layer, g[t], 0, 0, 0)),
                  pl.BlockSpec((None, None, e, D_MODEL, D_EXPERT), lambda t, s, d, g: (layer, g[t], 0, 0, 0)),
                  pl.BlockSpec((None, None, e, D_EXPERT, D_MODEL), lambda t, s, d, g: (layer, g[t], 0, 0, 0))],
        out_specs=pl.BlockSpec(memory_space=pl.ANY),
        scratch_shapes=[pltpu.VMEM((2, tm, hx.shape[1]), F32), pltpu.VMEM((2, tm, D_MODEL), F32),
                        pltpu.VMEM((tm, D_MODEL), BF16), pltpu.SemaphoreType.DMA((2,)), pltpu.SemaphoreType.DMA((2,))])
    return pl.pallas_call(
        _moe_kernel, grid_spec=gs,
        out_shape=jax.ShapeDtypeStruct((n + 2 * tm, D_MODEL), F32),
        compiler_params=_params(1), name="moe_experts",
    )(src, dst, tile_group, hx, w1, w3, w2)


def _route_tables(gidx, n):
    tm = MOE_TILE
    n_tiles = (n + N_GROUPS * (tm - 1)) // tm
    n_slots = n_tiles * tm
    onehot = (gidx[:, None] == jnp.arange(N_GROUPS, dtype=I32)[None, :]).astype(I32)
    counts = jnp.sum(onehot, axis=0)
    padded = ((counts + tm - 1) // tm) * tm
    ends = jnp.cumsum(padded)
    starts = ends - padded
    rank = jnp.cumsum(onehot, axis=0) - onehot
    pos = starts[gidx] + jnp.sum(rank * onehot, axis=1)
    token = jnp.full((n_slots,), -1, I32).at[pos].set(jnp.arange(n, dtype=I32))
    slot = jnp.arange(n_slots, dtype=I32)
    spare = n + ((slot // tm) % 2) * tm + slot % tm
    src = jnp.maximum(token, 0)
    dst = jnp.concatenate([n + tm + jnp.arange(tm, dtype=I32), jnp.where(token >= 0, token, spare)])
    tile_start = jnp.arange(n_tiles, dtype=I32) * tm
    tile_group = jnp.minimum(jnp.sum((tile_start[:, None] >= ends[None, :]).astype(I32), axis=1), N_GROUPS - 1)
    return src, dst, tile_group.astype(I32)


def _final_ln_kernel(x_ref, y_ref, g_ref, lg_ref, lb_ref, o_ref):
    z = ALPHA * x_ref[...] + (1.0 + g_ref[...]) * y_ref[...]
    o_ref[...] = _layer_norm(z, lg_ref[...], lb_ref[...])


def _final_ln(x, y, mod, rows_per_seq, lg, lb):
    n = x.shape[0]
    tm = _row_tile(n)
    row = pl.BlockSpec((tm, D_MODEL), lambda i: (i, 0))
    return pl.pallas_call(
        _final_ln_kernel,
        grid=(n // tm,),
        in_specs=[row, row, _mod_spec(mod, 5, tm, rows_per_seq), _const_spec(lg.shape), _const_spec(lb.shape)],
        out_specs=row,
        out_shape=jax.ShapeDtypeStruct((n, D_MODEL), F32),
        compiler_params=_params(1), name="final_ln",
    )(x, y, mod, lg, lb)


def _trunk(x3, mods, past, wts):
    batch, seq, _ = x3.shape
    n = batch * seq
    x = x3.reshape(n, D_MODEL)
    fk, fv, fl, cb, dk, dv = [], [], [], [], [], []
    if past is not None:
        c_fk, c_fv, c_fl, s_conv, c_dk, c_dv, page_table = past
        n_pages = page_table.shape[1]
        pt_flat = page_table.reshape(-1).astype(I32)
        fk_t = jnp.transpose(c_fk, (0, 1, 3, 4, 2))
        fv_t = jnp.transpose(c_fv, (0, 1, 3, 4, 2))
        fl_t = jnp.transpose(c_fl, (0, 1, 3, 2))
        dk_t = jnp.transpose(c_dk, (0, 1, 3, 4, 2))
        dv_rows = c_dv.reshape(c_dv.shape[0], c_dv.shape[1], -1, c_dv.shape[-1])
        per_seq = lambda a: a.reshape(batch, seq, a.shape[-1])
    for l in range(DEPTH):
        i = l // 2
        mod = mods[l]
        if l % 2 == 0:
            qb, kb, vb, k, v, gb, cu, lf = _even_in(x, mod, seq, wts["wm"][i], wts["wf"][i], wts["bf"][i])
            if past is None:
                logf, ctok, ct = _cumsum(lf, batch, seq)
                o_fox = _flash(qb, kb, vb, batch, seq, fox=True, extra=(ctok, ct))
                buf = jnp.zeros((batch, CONV_WIDTH - 1, CONV_DIM), F32)
            else:
                logf = lf[:, :N_HEADS]
                lnt = jnp.pad(jnp.transpose(per_seq(logf), (0, 2, 1)), ((0, 0), (0, 0), (0, LANES - seq)))
                o_fox = _fox_decode(i, pt_flat, n_pages, per_seq(qb), per_seq(k), per_seq(v), lnt,
                                    fk_t, fv_t, fl_t).reshape(n, FOX_DIM)
                buf = s_conv[i]
            o_conv, new_buf = _conv(cu.reshape(batch, seq, CONV_DIM), gb.reshape(batch, seq, CONV_DIM), buf,
                                    wts["conv_w"][i])
            x = _out_ln([o_fox, o_conv.reshape(n, CONV_DIM)], wts["w_out_even"][i], x, mod, seq,
                        wts["ln_g"][l][0], wts["ln_b"][l][0])
            fk.append(k.reshape(batch, seq, N_HEADS, HEAD_DIM))
            fv.append(v.reshape(batch, seq, N_HEADS, HEAD_DIM))
            fl.append(logf.reshape(batch, seq, N_HEADS))
            cb.append(new_buf)
        else:
            lam_init = 0.8 - 0.6 * math.exp(-0.3 * l)
            qb, kb, vb, k, v = _odd_in(x, mod, seq, wts["w_qkv"][i])
            lam_vecs = [wts[name][i] for name in ("lq1", "lk1", "lq2", "lk2")]
            if past is None:
                o = _flash(qb, kb, vb, batch, seq, fox=False, extra=lam_vecs + [wts["subln"][i]], lam_init=lam_init)
            else:
                o = _diff_decode(i, pt_flat, n_pages, per_seq(qb), per_seq(k), per_seq(v), dk_t, dv_rows,
                                 lam_vecs, wts["subln"][i], lam_init).reshape(n, D_MODEL)
            x = _out_ln([o], wts["w_out_odd"][i], x, mod, seq, wts["ln_g"][l][0], wts["ln_b"][l][0])
            dk.append(k.reshape(batch, seq, 2 * N_HEADS, HEAD_DIM))
            dv.append(v.reshape(batch, seq, N_HEADS, 2 * HEAD_DIM))
        hx, gidx = _router(x, mod, seq, wts["wr"][l], wts["br"][l])
        src, dst, tile_group = _route_tables(gidx[:, 0], n)
        y = _moe(hx, src, dst, tile_group, wts["w1"], wts["w3"], wts["w2"], l)
        x = _final_ln(x, y, mod, seq, wts["ln_g"][l][1], wts["ln_b"][l][1])
    return (x.reshape(batch, seq, D_MODEL), jnp.stack(fk), jnp.stack(fv), jnp.stack(fl), jnp.stack(cb),
            jnp.stack(dk), jnp.stack(dv))


def _prepare_weights(ln_g, ln_b, w_in_even, b_forget, conv_w, w_out_even, w_qkv_odd, lambda_q1, lambda_k1,
                     lambda_q2, lambda_k2, subln_g, w_out_odd, w_gr, b_gr, w_er, b_er, w1, w3, w2):
    f0, f1 = 3 * FOX_DIM, 3 * FOX_DIM + N_HEADS
    n_exp = N_GROUPS * EXPERTS_PER_GROUP
    pad_lanes = lambda a: jnp.pad(a, [(0, 0)] * (a.ndim - 1) + [(0, LANES - a.shape[-1])])
    wr = jnp.concatenate([w_gr, w_er.transpose(0, 2, 1, 3).reshape(DEPTH, D_MODEL, n_exp)], axis=-1)
    br = jnp.concatenate([b_gr, b_er.reshape(DEPTH, n_exp)], axis=-1)
    grouped = lambda w: w.astype(BF16).reshape(DEPTH, N_GROUPS, EXPERTS_PER_GROUP, w.shape[2], w.shape[3])
    row = lambda a: a[:, None, :]
    return dict(
        ln_g=ln_g[:, :, None, :], ln_b=ln_b[:, :, None, :],
        wm=jnp.concatenate([w_in_even[:, :, :f0], w_in_even[:, :, f1:]], axis=-1).astype(BF16),
        wf=pad_lanes(w_in_even[:, :, f0:f1]).astype(BF16), bf=pad_lanes(b_forget)[:, None, :],
        conv_w=conv_w, w_out_even=w_out_even.astype(BF16), w_qkv=w_qkv_odd.astype(BF16),
        lq1=row(lambda_q1), lk1=row(lambda_k1), lq2=row(lambda_q2), lk2=row(lambda_k2), subln=row(subln_g),
        w_out_odd=w_out_odd.astype(BF16), wr=pad_lanes(wr), br=pad_lanes(br)[:, None, :],
        w1=grouped(w1), w3=grouped(w3), w2=grouped(w2))


def kernel(x_prompt, x_sample, cache_fox_k, cache_fox_v, cache_fox_logf, state_conv, cache_diff_k, cache_diff_v,
           page_table, c_prompt, c_sample, ada_w, ada_b, ln_g, ln_b, w_in_even, b_forget, conv_w, w_out_even,
           w_qkv_odd, lambda_q1, lambda_k1, lambda_q2, lambda_k2, subln_g, w_out_odd, w_gr, b_gr, w_er, b_er,
           w1, w3, w2):
    wts = _prepare_weights(ln_g, ln_b, w_in_even, b_forget, conv_w, w_out_even, w_qkv_odd, lambda_q1, lambda_k1,
                           lambda_q2, lambda_k2, subln_g, w_out_odd, w_gr, b_gr, w_er, b_er, w1, w3, w2)
    nbp, nbs = c_prompt.shape[0], c_sample.shape[0]
    n_rows = nbp + nbs
    pad = (-n_rows) % SUBLANES
    c_all = jnp.concatenate([c_prompt, c_sample, jnp.zeros((pad, D_MODEL), F32)], axis=0)
    mod_all = _adaln(c_all, ada_w, ada_b)
    mods_p = [mod_all[l, :nbp][:, None, :] for l in range(DEPTH)]
    mods_s = [jnp.repeat(mod_all[l, nbp:n_rows], x_sample.shape[1], axis=0) for l in range(DEPTH)]
    y_p, fk_p, fv_p, fl_p, cb_p, dk_p, dv_p = _trunk(x_prompt, mods_p, None, wts)
    past = (cache_fox_k, cache_fox_v, cache_fox_logf, state_conv, cache_diff_k, cache_diff_v, page_table)
    y_s, fk_s, fv_s, fl_s, cb_s, dk_s, dv_s = _trunk(x_sample, mods_s, past, wts)
    return (y_p, y_s, fk_p, fv_p, fl_p, cb_p, dk_p, dv_p, fk_s, fv_s, fl_s, cb_s, dk_s, dv_s)
```
